```python
import jax, jax.numpy as jnp
from jax import lax
import numpy as np

D_MODEL = 2048
BATCH = 1
SEQ = 8192
DEPTH = 4

GRID_W = 64
CTX_LEN = 256
D_MIX = D_MODEL
D_FF = ((8 * D_MODEL // 3 + 255) // 256) * 256
N_MOD = 9
EPS = 1e-6
ROPE_THETA = 10000.0
SGU_WIDTH = D_MIX // 4
SGU_GROUPS = 4
SGU_CH = SGU_WIDTH // SGU_GROUPS
SGU_CHUNK = 128
NA_WIDTH = D_MIX // 4
NA_HEADS = 8
NA_HD = NA_WIDTH // NA_HEADS
NA_WIN_H = 8
NA_WIN_W = 16
MLSTM_WIDTH = D_MIX - SGU_WIDTH - NA_WIDTH
MLSTM_HEADS = 8
MLSTM_HD = MLSTM_WIDTH // MLSTM_HEADS
MLSTM_CHUNK = 128
MLSTM_CONV_W = 5
IN_SIZES = (SGU_WIDTH, SGU_WIDTH, MLSTM_WIDTH, MLSTM_WIDTH, MLSTM_WIDTH, MLSTM_WIDTH, 4 * MLSTM_HEADS, NA_WIDTH, NA_WIDTH, NA_WIDTH)
D_IN = sum(IN_SIZES)
IN_SPLITS = tuple(int(s) for s in np.cumsum(IN_SIZES)[:-1])

kernel_name = 'hybrid_sgu_mlstm_natten_macaron_dit'


def rmsnorm(x, g):
    xf = x.astype(jnp.float32)
    y = xf * lax.rsqrt(jnp.mean(xf * xf, axis=-1, keepdims=True) + EPS)
    return (y * g.astype(jnp.float32)).astype(x.dtype)


def modulate(x, shift, scale):
    return x * (1 + scale) + shift


def swiglu(x, w13, w2):
    a, g = jnp.split(x @ w13, 2, axis=-1)
    return (jax.nn.silu(a) * g) @ w2


def spatial_gating(u, v, norm_g, w_s, b_s):
    B, T, _ = u.shape
    u, v = jax.nn.gelu(u), jax.nn.gelu(v)
    v = rmsnorm(v, norm_g).reshape(B, T // SGU_CHUNK, SGU_CHUNK, SGU_GROUPS, SGU_CH)
    mixed = jnp.einsum('gts,bnsgc->bntgc', w_s, v) + b_s.T[None, None, :, :, None]
    return u * mixed.reshape(B, T, SGU_WIDTH)


def dwconv(x, w, b):
    y = lax.conv_general_dilated(x, w[:, None, :], window_strides=(1,), padding='SAME',
                                 dimension_numbers=('NWC', 'WIO', 'NWC'), feature_group_count=x.shape[-1])
    return y + b


def axial_rope(x, row, col):
    half = x.shape[-1] // 2
    nf = half // 2
    freqs = ROPE_THETA ** (-jnp.arange(nf, dtype=jnp.float32) / nf)

    def rot(xa, pos):
        ang = pos[:, None] * freqs
        cos = jnp.cos(ang)[None, :, None, :].astype(x.dtype)
        sin = jnp.sin(ang)[None, :, None, :].astype(x.dtype)
        x1, x2 = xa[..., :nf], xa[..., nf:]
        return jnp.concatenate([x1 * cos - x2 * sin, x1 * sin + x2 * cos], axis=-1)

    return jnp.concatenate([rot(x[..., :half], row), rot(x[..., half:], col)], axis=-1)


def mlstm_zero_state(B):
    return (jnp.zeros((B, MLSTM_HEADS, MLSTM_HD, MLSTM_HD), jnp.float32),
            jnp.zeros((B, MLSTM_HEADS, MLSTM_HD), jnp.float32),
            jnp.zeros((B, MLSTM_HEADS), jnp.float32))


def mlstm_chunkwise(q, k, v, i_pre, f_pre, state):
    B, H, T, d = q.shape
    L = MLSTM_CHUNK
    nc = T // L

    def to_chunks(a):
        return jnp.moveaxis(a.reshape(a.shape[:2] + (nc, L) + a.shape[3:]), 2, 0)

    xs = tuple(to_chunks(a.astype(jnp.float32)) for a in (q, k, v, i_pre, jax.nn.log_sigmoid(f_pre.astype(jnp.float32))))
    tril = jnp.tril(jnp.ones((L, L), dtype=bool))

    def step(carry, inp):
        C, n, m = carry
        q_c, k_c, v_c, i_c, lf_c = inp
        b = jnp.cumsum(lf_c, axis=-1)
        d_mat = jnp.where(tril, b[..., :, None] - b[..., None, :] + i_c[..., None, :], -jnp.inf)
        inter = b + m[..., None]
        m_t = jnp.maximum(jnp.max(d_mat, axis=-1), inter)
        s = jnp.einsum('bhtd,bhsd->bhts', q_c, k_c) * jnp.exp(d_mat - m_t[..., None])
        a = jnp.exp(inter - m_t)
        num = a[..., None] * jnp.einsum('bhtd,bhde->bhte', q_c, C) + jnp.einsum('bhts,bhse->bhte', s, v_c)
        den = a * jnp.einsum('bhtd,bhd->bht', q_c, n) + jnp.sum(s, axis=-1)
        h = num / jnp.maximum(jnp.abs(den), jnp.exp(-m_t))[..., None]
        b_last = b[..., -1]
        g = b_last[..., None] - b + i_c
        m_new = jnp.maximum(b_last + m, jnp.max(g, axis=-1))
        w_k = jnp.exp(g - m_new[..., None])
        decay = jnp.exp(b_last + m - m_new)
        C = decay[..., None, None] * C + jnp.einsum('bhs,bhsd,bhse->bhde', w_k, k_c, v_c)
        n = decay[..., None] * n + jnp.einsum('bhs,bhsd->bhd', w_k, k_c)
        return (C, n, m_new), h

    state, h = lax.scan(step, state, xs)
    h = jnp.moveaxis(h, 0, 2).reshape(B, H, T, d)
    return h, state


def mlstm_prep(q, k, v, gates, conv_w, conv_b, gate_b, pos):
    B, T, _ = q.shape
    qk = jax.nn.silu(dwconv(jnp.concatenate([q, k], axis=-1), conv_w, conv_b))
    q, k = jnp.split(qk.reshape(B, T, 2 * MLSTM_HEADS, MLSTM_HD), 2, axis=2)
    if pos is not None:
        q, k = axial_rope(q, *pos), axial_rope(k, *pos)
    q = q * MLSTM_HD ** -0.5
    v = v.reshape(B, T, MLSTM_HEADS, MLSTM_HD)
    g = (gates.reshape(B, T, 4, MLSTM_HEADS).astype(jnp.float32) + gate_b.astype(jnp.float32)).transpose(2, 0, 3, 1)
    heads = lambda a: a.transpose(0, 2, 1, 3)
    return heads(q), heads(k), heads(v), g


def bidirectional_mlstm(lat, ctx):
    ql, kl, vl, gl = lat
    qc, kc, vc, gc = ctx
    flip = lambda a: jnp.flip(a, axis=2)
    zero = mlstm_zero_state(ql.shape[0])
    hc_f, st_f = mlstm_chunkwise(qc, kc, vc, gc[0], gc[1], zero)
    hl_f, _ = mlstm_chunkwise(ql, kl, vl, gl[0], gl[1], st_f)
    hc_b, st_b = mlstm_chunkwise(flip(qc), flip(kc), flip(vc), flip(gc[2]), flip(gc[3]), zero)
    hl_b, _ = mlstm_chunkwise(flip(ql), flip(kl), flip(vl), flip(gl[2]), flip(gl[3]), st_b)
    return hl_f + flip(hl_b), hc_f + flip(hc_b)


def mlstm_out(h, o, norm_g):
    B, H, T, d = h.shape
    h = rmsnorm(h.transpose(0, 2, 1, 3), norm_g.reshape(H, d)).reshape(B, T, H * d)
    return (jax.nn.sigmoid(o) * h).astype(o.dtype)


def na_heads(a):
    B, T, _ = a.shape
    return a.reshape(B, T, NA_HEADS, NA_HD)


def neighborhood_attention(q, k, v, k_ctx, v_ctx, rpb, rows):
    B, T, H, d = q.shape
    win_h = min(NA_WIN_H, rows)
    nk = win_h * NA_WIN_W
    qg = q.reshape(B, rows, GRID_W, H, d)
    kg = k.reshape(B, rows, GRID_W, H, d)
    vg = v.reshape(B, rows, GRID_W, H, d)
    cols = jnp.arange(GRID_W)
    col_idx = jnp.clip(cols - NA_WIN_W // 2, 0, GRID_W - NA_WIN_W)[:, None] + jnp.arange(NA_WIN_W)
    dc = col_idx - cols[:, None] + (NA_WIN_W - 1)
    r = jnp.arange(rows)
    row_start = jnp.clip(r - win_h // 2, 0, rows - win_h)
    dr = row_start[:, None] + jnp.arange(win_h) - r[:, None] + (NA_WIN_H - 1)
    scale = d ** -0.5

    def row_block(args):
        q_r, rs, dr_r = args
        k_r = lax.dynamic_slice_in_dim(kg, rs, win_h, axis=1)[:, :, col_idx]
        v_r = lax.dynamic_slice_in_dim(vg, rs, win_h, axis=1)[:, :, col_idx]
        bias = rpb[:, dr_r[:, None, None], dc[None]].transpose(0, 2, 1, 3).reshape(H, GRID_W, nk)
        s_loc = jnp.einsum('bqhd,bjqwhd->bhqjw', q_r, k_r).reshape(B, H, GRID_W, nk) * scale + bias
        s_ctx = jnp.einsum('bqhd,bchd->bhqc', q_r, k_ctx) * scale
        p = jax.nn.softmax(jnp.concatenate([s_loc, s_ctx], axis=-1).astype(jnp.float32), axis=-1).astype(v.dtype)
        p_loc = p[..., :nk].reshape(B, H, GRID_W, win_h, NA_WIN_W)
        return jnp.einsum('bhqjw,bjqwhd->bqhd', p_loc, v_r) + jnp.einsum('bhqc,bchd->bqhd', p[..., nk:], v_ctx)

    out = lax.map(row_block, (jnp.moveaxis(qg, 1, 0), row_start, dr))
    return jnp.moveaxis(out, 0, 1).reshape(B, T, H * d)


def context_attention(q, k, v):
    B, T, H, d = q.shape
    s = jnp.einsum('bqhd,bkhd->bhqk', q, k) * d ** -0.5
    p = jax.nn.softmax(s.astype(jnp.float32), axis=-1).astype(v.dtype)
    return jnp.einsum('bhqk,bkhd->bqhd', p, v).reshape(B, T, H * d)


def token_mixer(h_lat, h_ctx, w_in, w_out, sgu_norm_g, sgu_w, sgu_b, conv_w, conv_b, gate_b,
                mlstm_norm_g, rpb, pos, rows, need_ctx):
    p_lat = jnp.split(h_lat @ w_in, IN_SPLITS, axis=-1)
    p_ctx = jnp.split(h_ctx @ w_in, IN_SPLITS, axis=-1)
    a_lat = spatial_gating(p_lat[0], p_lat[1], sgu_norm_g, sgu_w, sgu_b)
    m_lat = mlstm_prep(p_lat[2], p_lat[3], p_lat[4], p_lat[6], conv_w, conv_b, gate_b, pos)
    m_ctx = mlstm_prep(p_ctx[2], p_ctx[3], p_ctx[4], p_ctx[6], conv_w, conv_b, gate_b, None)
    hm_lat, hm_ctx = bidirectional_mlstm(m_lat, m_ctx)
    b_lat = mlstm_out(hm_lat, p_lat[5], mlstm_norm_g)
    k_ctx, v_ctx = na_heads(p_ctx[8]), na_heads(p_ctx[9])
    c_lat = neighborhood_attention(na_heads(p_lat[7]), na_heads(p_lat[8]), na_heads(p_lat[9]), k_ctx, v_ctx, rpb, rows)
    y_lat = jnp.concatenate([a_lat, b_lat, c_lat], axis=-1) @ w_out
    if not need_ctx:
        return y_lat, None
    a_ctx = spatial_gating(p_ctx[0], p_ctx[1], sgu_norm_g, sgu_w, sgu_b)
    b_ctx = mlstm_out(hm_ctx, p_ctx[5], mlstm_norm_g)
    c_ctx_out = context_attention(na_heads(p_ctx[7]), k_ctx, v_ctx)
    y_ctx = jnp.concatenate([a_ctx, b_ctx, c_ctx_out], axis=-1) @ w_out
    return y_lat, y_ctx


def setup_inputs(seed: int = 0) -> dict:
    key = jax.random.key(seed)
    ks = jax.random.split(key, 24)
    nrm = lambda k, shape, s: jax.random.normal(k, shape, jnp.float32) * s
    D, L = D_MODEL, DEPTH
    f_base = jnp.linspace(3.0, 6.0, MLSTM_HEADS, dtype=jnp.float32)
    i_base = jnp.zeros((MLSTM_HEADS,), jnp.float32)
    gate_base = jnp.stack([i_base, f_base, i_base, f_base])
    return {
        'x': nrm(ks[0], (BATCH, SEQ, D), 1.0),
        'c': nrm(ks[1], (BATCH, D), 1.0),
        'ctx': nrm(ks[2], (BATCH, CTX_LEN, D), 1.0),
        'c_ctx': nrm(ks[3], (D,), 1.0),
        'ada_w': nrm(ks[4], (L, D, N_MOD * D), 0.5 * D ** -0.5),
        'ada_b': nrm(ks[5], (L, N_MOD * D), 0.02),
        'norm_g': 1.0 + nrm(ks[6], (L, 3, D), 0.05),
        'ffn1_w13': nrm(ks[7], (L, D, 2 * D_FF), D ** -0.5),
        'ffn1_w2': nrm(ks[8], (L, D_FF, D), D_FF ** -0.5),
        'ffn2_w13': nrm(ks[9], (L, D, 2 * D_FF), D ** -0.5),
        'ffn2_w2': nrm(ks[10], (L, D_FF, D), D_FF ** -0.5),
        'mix_w_in': nrm(ks[11], (L, D, D_IN), D ** -0.5),
        'mix_w_out': nrm(ks[12], (L, D_MIX, D), D_MIX ** -0.5),
        'sgu_norm_g': 1.0 + nrm(ks[13], (L, SGU_WIDTH), 0.05),
        'sgu_w': nrm(ks[14], (L, SGU_GROUPS, SGU_CHUNK, SGU_CHUNK), 0.5 * SGU_CHUNK ** -0.5),
        'sgu_b': 1.0 + nrm(ks[15], (L, SGU_GROUPS, SGU_CHUNK), 0.1),
        'mlstm_conv_w': nrm(ks[16], (L, MLSTM_CONV_W, 2 * MLSTM_WIDTH), MLSTM_CONV_W ** -0.5),
        'mlstm_conv_b': nrm(ks[17], (L, 2 * MLSTM_WIDTH), 0.02),
        'mlstm_gate_b': gate_base + nrm(ks[18], (L, 4, MLSTM_HEADS), 0.1),
        'mlstm_norm_g': 1.0 + nrm(ks[19], (L, MLSTM_WIDTH), 0.05),
        'na_rpb': nrm(ks[20], (L, NA_HEADS, 2 * NA_WIN_H - 1, 2 * NA_WIN_W - 1), 0.1),
        'final_norm_g': 1.0 + nrm(ks[21], (D,), 0.05),
    }


def reference(x, c, ctx, c_ctx, ada_w, ada_b, norm_g, ffn1_w13, ffn1_w2, ffn2_w13, ffn2_w2,
              mix_w_in, mix_w_out, sgu_norm_g, sgu_w, sgu_b, mlstm_conv_w, mlstm_conv_b,
              mlstm_gate_b, mlstm_norm_g, na_rpb, final_norm_g):
    B, T, _ = x.shape
    rows = T // GRID_W
    t = jnp.arange(T)
    pos = ((t // GRID_W).astype(jnp.float32), (t % GRID_W).astype(jnp.float32))
    h_lat, h_ctx = x, ctx
    for l in range(DEPTH):
        last = l == DEPTH - 1
        ml = jnp.split((jax.nn.silu(c) @ ada_w[l] + ada_b[l])[:, None, :], N_MOD, axis=-1)
        mc = jnp.split(jax.nn.silu(c_ctx) @ ada_w[l] + ada_b[l], N_MOD, axis=-1)
        ng = norm_g[l]
        h_lat = h_lat + 0.5 * ml[2] * swiglu(modulate(rmsnorm(h_lat, ng[0]), ml[0], ml[1]), ffn1_w13[l], ffn1_w2[l])
        h_ctx = h_ctx + 0.5 * mc[2] * swiglu(modulate(rmsnorm(h_ctx, ng[0]), mc[0], mc[1]), ffn1_w13[l], ffn1_w2[l])
        y_lat, y_ctx = token_mixer(modulate(rmsnorm(h_lat, ng[1]), ml[3], ml[4]),
                                   modulate(rmsnorm(h_ctx, ng[1]), mc[3], mc[4]),
                                   mix_w_in[l], mix_w_out[l], sgu_norm_g[l], sgu_w[l], sgu_b[l],
                                   mlstm_conv_w[l], mlstm_conv_b[l], mlstm_gate_b[l], mlstm_norm_g[l],
                                   na_rpb[l], pos, rows, not last)
        h_lat = h_lat + ml[5] * y_lat
        h_lat = h_lat + 0.5 * ml[8] * swiglu(modulate(rmsnorm(h_lat, ng[2]), ml[6], ml[7]), ffn2_w13[l], ffn2_w2[l])
        if not last:
            h_ctx = h_ctx + mc[5] * y_ctx
            h_ctx = h_ctx + 0.5 * mc[8] * swiglu(modulate(rmsnorm(h_ctx, ng[2]), mc[6], mc[7]), ffn2_w13[l], ffn2_w2[l])
    return rmsnorm(h_lat, final_norm_g)
```

```python
import functools

import numpy as np
import jax
import jax.numpy as jnp
from jax import lax
from jax.experimental import pallas as pl
from jax.experimental.pallas import tpu as pltpu

F32 = jnp.float32
BF16 = jnp.bfloat16

D_MODEL = 2048
SEQ = 8192
DEPTH = 4
GRID_W = 64
CTX_LEN = 256
N_TOK = CTX_LEN + SEQ
D_FF = 5632
N_MOD = 9
EPS = 1e-6
ROPE_THETA = 10000.0
SGU_WIDTH = 512
SGU_GROUPS = 4
SGU_CH = SGU_WIDTH // SGU_GROUPS
SGU_CHUNK = 128
NA_WIDTH = 512
NA_HEADS = 8
NA_HD = NA_WIDTH // NA_HEADS
NA_WIN_H = 8
NA_WIN_W = 16
MLSTM_WIDTH = 1024
MLSTM_HEADS = 8
MLSTM_HD = MLSTM_WIDTH // MLSTM_HEADS
MLSTM_CHUNK = 128
MLSTM_CONV_W = 5
GRID_ROWS = SEQ // GRID_W
P_COLS = 2 * SGU_WIDTH + 4 * MLSTM_WIDTH + 3 * NA_WIDTH
GATE_PAD = 128
NEG_BIG = -1e30

VMEM_LIMIT = 60 * 1024 * 1024
TM_FFN = 528
TM_IN = 768
TM_OUT = 384
TF = 512
TN_IN = 512
TM_PREP = 256
HALO = 8


def _cparams(n_axes):
    return pltpu.CompilerParams(dimension_semantics=("arbitrary",) * n_axes,
                                vmem_limit_bytes=VMEM_LIMIT)


def _sigmoid(x):
    return 1.0 / (1.0 + jnp.exp(-x))


def _rms(x, g):
    return x * lax.rsqrt(jnp.mean(x * x, axis=-1, keepdims=True) + EPS) * g


def _is_ctx_rows(row0, tm):
    rows = row0 + lax.broadcasted_iota(jnp.int32, (tm, 1), 0)
    return rows < CTX_LEN


def _norm_modulate(x, g, mod_ref, is_ctx):
    y = _rms(x, g)
    shift = jnp.where(is_ctx, mod_ref[3:4, :], mod_ref[0:1, :])
    scale = jnp.where(is_ctx, mod_ref[4:5, :], mod_ref[1:2, :])
    return y * (1.0 + scale) + shift


def _ada_kernel(x_ref, w_ref, b_ref, o_ref):
    x = x_ref[...]
    xs = (x * _sigmoid(x)).astype(BF16)
    o_ref[0] = jnp.dot(xs, w_ref[0].astype(BF16), preferred_element_type=F32) + b_ref[0]


def _ada_call(cc, ada_w, ada_b):
    depth, d, nd = ada_w.shape
    tn = 1024
    return pl.pallas_call(
        _ada_kernel,
        grid=(depth, nd // tn),
        in_specs=[pl.BlockSpec((8, d), lambda l, j: (0, 0)),
                  pl.BlockSpec((1, d, tn), lambda l, j: (l, 0, j)),
                  pl.BlockSpec((1, 1, tn), lambda l, j: (l, 0, j))],
        out_specs=pl.BlockSpec((1, 8, tn), lambda l, j: (l, 0, j)),
        out_shape=jax.ShapeDtypeStruct((depth, 8, nd), F32),
        compiler_params=_cparams(2),
        name="ada_proj",
    )(cc, ada_w, ada_b.reshape(depth, 1, nd))


def _ffn_kernel(x_ref, g_ref, mod_ref, w1_ref, w3_ref, w2_ref, o_ref, xn_ref, *, tm, nk):
    i = pl.program_id(0)
    k = pl.program_id(1)
    is_ctx = _is_ctx_rows(i * tm, tm)

    @pl.when(k == 0)
    def _():
        xn_ref[...] = _norm_modulate(x_ref[...], g_ref[...], mod_ref, is_ctx).astype(BF16)

    xn = xn_ref[...]
    a = jnp.dot(xn, w1_ref[...], preferred_element_type=F32)
    g = jnp.dot(xn, w3_ref[...], preferred_element_type=F32)
    act = (a * _sigmoid(a) * g).astype(BF16)
    part = jnp.dot(act, w2_ref[...], preferred_element_type=F32)

    @pl.when(k == 0)
    def _():
        o_ref[...] = part

    @pl.when(k > 0)
    def _():
        o_ref[...] += part

    @pl.when(k == nk - 1)
    def _():
        gate = jnp.where(is_ctx, mod_ref[5:6, :], mod_ref[2:3, :])
        o_ref[...] = x_ref[...] + 0.5 * gate * o_ref[...]


def _ffn_call(h, g, mod, w13, w2):
    n, d = h.shape
    nk = D_FF // TF
    tm = TM_FFN
    return pl.pallas_call(
        functools.partial(_ffn_kernel, tm=tm, nk=nk),
        grid=(n // tm, nk),
        in_specs=[pl.BlockSpec((tm, d), lambda i, k: (i, 0)),
                  pl.BlockSpec((1, d), lambda i, k: (0, 0)),
                  pl.BlockSpec((8, d), lambda i, k: (0, 0)),
                  pl.BlockSpec((d, TF), lambda i, k: (0, k)),
                  pl.BlockSpec((d, TF), lambda i, k: (0, k + nk)),
                  pl.BlockSpec((TF, d), lambda i, k: (k, 0))],
        out_specs=pl.BlockSpec((tm, d), lambda i, k: (i, 0)),
        out_shape=jax.ShapeDtypeStruct((n, d), F32),
        scratch_shapes=[pltpu.VMEM((tm, d), BF16)],
        compiler_params=_cparams(2),
        name="ffn",
    )(h, g, mod, w13, w13, w2)


def _inproj_kernel(x_ref, g_ref, mod_ref, w_ref, wg_ref, p_ref, gates_ref, xn_ref, *, tm):
    i = pl.program_id(0)
    j = pl.program_id(1)

    @pl.when(j == 0)
    def _():
        is_ctx = _is_ctx_rows(i * tm, tm)
        xn = _norm_modulate(x_ref[...], g_ref[...], mod_ref, is_ctx).astype(BF16)
        xn_ref[...] = xn
        gates_ref[...] = jnp.dot(xn, wg_ref[...], preferred_element_type=F32)

    p_ref[...] = jnp.dot(xn_ref[...], w_ref[...], preferred_element_type=F32)


def _inproj_call(h, g, mod, w_main, w_gate):
    n, d = h.shape
    tm = TM_IN
    return pl.pallas_call(
        functools.partial(_inproj_kernel, tm=tm),
        grid=(n // tm, P_COLS // TN_IN),
        in_specs=[pl.BlockSpec((tm, d), lambda i, j: (i, 0)),
                  pl.BlockSpec((1, d), lambda i, j: (0, 0)),
                  pl.BlockSpec((8, d), lambda i, j: (0, 0)),
                  pl.BlockSpec((d, TN_IN), lambda i, j: (0, j)),
                  pl.BlockSpec((d, GATE_PAD), lambda i, j: (0, 0))],
        out_specs=[pl.BlockSpec((tm, TN_IN), lambda i, j: (i, j)),
                   pl.BlockSpec((tm, GATE_PAD), lambda i, j: (i, 0))],
        out_shape=[jax.ShapeDtypeStruct((n, P_COLS), F32),
                   jax.ShapeDtypeStruct((n, GATE_PAD), F32)],
        scratch_shapes=[pltpu.VMEM((tm, d), BF16)],
        compiler_params=_cparams(2),
        name="mix_inproj",
    )(h, g, mod, w_main, w_gate)


def _gelu_tanh(x):
    c = np.float32(np.sqrt(2.0 / np.pi))
    return x * (0.5 * (1.0 + jnp.tanh(c * (x + 0.044715 * (x * x * x)))))


def _sgu_kernel(u_ref, v_ref, g_ref, w_ref, b_ref, o_ref):
    u = _gelu_tanh(u_ref[...])
    v = _rms(_gelu_tanh(v_ref[...]), g_ref[...]).astype(BF16)
    for grp in range(SGU_GROUPS):
        sl = slice(grp * SGU_CH, (grp + 1) * SGU_CH)
        mixed = jnp.dot(w_ref[grp].astype(BF16), v[:, sl], preferred_element_type=F32) + b_ref[:, sl]
        o_ref[:, sl] = (u[:, sl] * mixed).astype(o_ref.dtype)


def _sgu_call(p, norm_g, w_s, b_full):
    n = p.shape[0]
    c = SGU_CHUNK
    return pl.pallas_call(
        _sgu_kernel,
        grid=(n // c,),
        in_specs=[pl.BlockSpec((c, SGU_WIDTH), lambda i: (i, 0)),
                  pl.BlockSpec((c, SGU_WIDTH), lambda i: (i, 1)),
                  pl.BlockSpec((1, SGU_WIDTH), lambda i: (0, 0)),
                  pl.BlockSpec((SGU_GROUPS, c, c), lambda i: (0, 0, 0)),
                  pl.BlockSpec((c, SGU_WIDTH), lambda i: (0, 0))],
        out_specs=pl.BlockSpec((c, SGU_WIDTH), lambda i: (i, 0)),
        out_shape=jax.ShapeDtypeStruct((n, SGU_WIDTH), BF16),
        compiler_params=_cparams(1),
        name="sgu",
    )(p, p, norm_g, w_s, b_full)


def _prep_one(x_ref, prev_ref, next_ref, w_ref, b_ref, cos, sin, ext_ref, o_ref, zero_prev, zero_next,
              out_scale, tm):
    ext_ref[0:HALO, :] = jnp.where(zero_prev, 0.0, prev_ref[...])
    ext_ref[HALO:HALO + tm, :] = x_ref[...]
    ext_ref[HALO + tm:, :] = jnp.where(zero_next, 0.0, next_ref[...])
    half = MLSTM_CONV_W // 2
    y = b_ref[...] + w_ref[0:1, :] * ext_ref[HALO - half:HALO - half + tm, :]
    for j in range(1, MLSTM_CONV_W):
        y = y + w_ref[j:j + 1, :] * ext_ref[HALO - half + j:HALO - half + j + tm, :]
    y = y * _sigmoid(y)
    lane = lax.broadcasted_iota(jnp.int32, (tm, MLSTM_HD), 1)
    first = (lane % (MLSTM_HD // 2)) < (MLSTM_HD // 4)
    for hd in range(MLSTM_HEADS):
        sl = slice(hd * MLSTM_HD, (hd + 1) * MLSTM_HD)
        yh = y[:, sl]
        partner = jnp.where(first, pltpu.roll(yh, MLSTM_HD - MLSTM_HD // 4, 1),
                            pltpu.roll(yh, MLSTM_HD // 4, 1))
        o_ref[:, sl] = ((yh * cos + partner * sin) * out_scale).astype(o_ref.dtype)


def _mlstm_prep_kernel(q_ref, qp_ref, qn_ref, k_ref, kp_ref, kn_ref, wq_ref, wk_ref, bq_ref, bk_ref,
                       cos_ref, sin_ref, qo_ref, ko_ref, ext_ref, *, tm, n_tiles):
    i = pl.program_id(0)
    row0 = i * tm
    zero_prev = jnp.logical_or(row0 == 0, row0 == CTX_LEN)
    zero_next = jnp.logical_or(row0 + tm == CTX_LEN, i == n_tiles - 1)
    cos = cos_ref[...]
    sin = sin_ref[...]
    _prep_one(q_ref, qp_ref, qn_ref, wq_ref, bq_ref, cos, sin, ext_ref, qo_ref, zero_prev, zero_next,
              np.float32(MLSTM_HD ** -0.5), tm)
    _prep_one(k_ref, kp_ref, kn_ref, wk_ref, bk_ref, cos, sin, ext_ref, ko_ref, zero_prev, zero_next,
              np.float32(1.0), tm)


def _mlstm_prep_call(p, conv_w, conv_b, cos_t, sin_t):
    n = p.shape[0]
    tm = TM_PREP
    n_tiles = n // tm
    w = MLSTM_WIDTH
    hb = tm // HALO
    last_hb = n // HALO - 1
    q_col, k_col = 1, 2
    main = lambda col: pl.BlockSpec((tm, w), lambda i: (i, col))
    prev = lambda col: pl.BlockSpec((HALO, w), lambda i: (jnp.maximum(i * hb - 1, 0), col))
    nxt = lambda col: pl.BlockSpec((HALO, w), lambda i: (jnp.minimum((i + 1) * hb, last_hb), col))
    wspec = lambda col: pl.BlockSpec((MLSTM_CONV_W, w), lambda i: (0, col))
    bspec = lambda col: pl.BlockSpec((1, w), lambda i: (0, col))
    tab = pl.BlockSpec((tm, MLSTM_HD), lambda i: (i, 0))
    out = pl.BlockSpec((tm, w), lambda i: (i, 0))
    cb = conv_b.reshape(1, 2 * w)
    return pl.pallas_call(
        functools.partial(_mlstm_prep_kernel, tm=tm, n_tiles=n_tiles),
        grid=(n_tiles,),
        in_specs=[main(q_col), prev(q_col), nxt(q_col), main(k_col), prev(k_col), nxt(k_col),
                  wspec(0), wspec(1), bspec(0), bspec(1), tab, tab],
        out_specs=[out, out],
        out_shape=[jax.ShapeDtypeStruct((n, w), BF16), jax.ShapeDtypeStruct((n, w), BF16)],
        scratch_shapes=[pltpu.VMEM((tm + 2 * HALO, w), F32)],
        compiler_params=_cparams(1),
        name="mlstm_prep",
    )(p, p, p, p, p, p, conv_w, conv_w, cb, cb, cos_t, sin_t)


def _log_sigmoid(x):
    return jnp.minimum(x, 0.0) - jnp.log(1.0 + jnp.exp(-jnp.abs(x)))


def _mlstm_scan_kernel(q_ref, k_ref, v_ref, gc_ref, gr_ref, bc_ref, br_ref, o_ref, c_ref, n_ref, m_ref):
    d = pl.program_id(0)
    j = pl.program_id(1)
    L = MLSTM_CHUNK
    H = MLSTM_HEADS
    hd = MLSTM_HD

    @pl.when(j == 0)
    def _():
        c_ref[...] = jnp.zeros_like(c_ref)
        n_ref[...] = jnp.zeros_like(n_ref)
        m_ref[...] = jnp.zeros_like(m_ref)

    sgn = 1 - 2 * d
    r_idx = lax.broadcasted_iota(jnp.int32, (L, L), 0)
    c_idx = lax.broadcasted_iota(jnp.int32, (L, L), 1)
    tri = ((c_idx - r_idx) * sgn) <= 0
    tri_f = tri.astype(F32)
    tri_t_f = (((r_idx - c_idx) * sgn) <= 0).astype(F32)

    gc = gc_ref[0] + bc_ref[0]
    gr = gr_ref[0] + br_ref[0]
    i_col, lf_col = gc[:, 0:H], _log_sigmoid(gc[:, H:2 * H])
    i_row, lf_row = gr[0:H, :], _log_sigmoid(gr[H:2 * H, :])
    b_col = jnp.dot(tri_f, lf_col, preferred_element_type=F32, precision=lax.Precision.HIGHEST)
    b_row = jnp.dot(lf_row, tri_t_f, preferred_element_type=F32, precision=lax.Precision.HIGHEST)
    b_last = jnp.sum(lf_col, axis=0, keepdims=True)
    ib_row = i_row - b_row
    g_col = b_last - b_col + i_col
    g_max = jnp.max(g_col, axis=0, keepdims=True)

    for h in range(H):
        sl = slice(h * hd, (h + 1) * hd)
        q = q_ref[:, sl]
        k = k_ref[:, sl]
        v = v_ref[:, sl].astype(BF16)
        m_old = m_ref[h:h + 1, 0:1]
        c_old = c_ref[h]
        n_old = n_ref[h:h + 1, :]
        bc = b_col[:, h:h + 1]
        d_mat = jnp.where(tri, bc + ib_row[h:h + 1, :], -jnp.inf)
        inter = bc + m_old
        m_t = jnp.maximum(jnp.max(d_mat, axis=-1, keepdims=True), inter)
        qk = lax.dot_general(q, k, (((1,), (1,)), ((), ())), preferred_element_type=F32)
        s = qk * jnp.exp(d_mat - m_t)
        a = jnp.exp(inter - m_t)
        qc = jnp.dot(q, c_old.astype(BF16), preferred_element_type=F32)
        num = a * qc + jnp.dot(s.astype(BF16), v, preferred_element_type=F32)
        qn = jnp.sum(q.astype(F32) * n_old, axis=-1, keepdims=True)
        den = a * qn + jnp.sum(s, axis=-1, keepdims=True)
        o_ref[0, :, sl] = num * (1.0 / jnp.maximum(jnp.abs(den), jnp.exp(-m_t)))
        bl = b_last[:, h:h + 1]
        m_new = jnp.maximum(bl + m_old, g_max[:, h:h + 1])
        w_k = jnp.exp(g_col[:, h:h + 1] - m_new)
        decay = jnp.exp(bl + m_old - m_new)
        kw = k.astype(F32) * w_k
        c_ref[h] = decay * c_old + lax.dot_general(kw.astype(BF16), v, (((0,), (0,)), ((), ())),
                                                   preferred_element_type=F32)
        n_ref[h:h + 1, :] = decay * n_old + jnp.sum(kw, axis=0, keepdims=True)
        m_ref[h:h + 1, :] = jnp.broadcast_to(m_new, (1, hd))


def _mlstm_chunk_index(d, j, n_chunks):
    ctx_chunks = CTX_LEN // MLSTM_CHUNK
    bwd = jnp.where(j < ctx_chunks, ctx_chunks - 1 - j, n_chunks + ctx_chunks - 1 - j)
    return jnp.where(d == 0, j, bwd)


def _mlstm_scan_call(qp, kp, p, gcol, grow, bcol, brow):
    n = qp.shape[0]
    L = MLSTM_CHUNK
    nc = n // L
    w = MLSTM_WIDTH
    H2 = 2 * MLSTM_HEADS
    cidx = lambda d, j: _mlstm_chunk_index(d, j, nc)
    v_col = 3
    return pl.pallas_call(
        _mlstm_scan_kernel,
        grid=(2, nc),
        in_specs=[pl.BlockSpec((L, w), lambda d, j: (cidx(d, j), 0)),
                  pl.BlockSpec((L, w), lambda d, j: (cidx(d, j), 0)),
                  pl.BlockSpec((L, w), lambda d, j: (cidx(d, j), v_col)),
                  pl.BlockSpec((1, L, H2), lambda d, j: (d, cidx(d, j), 0)),
                  pl.BlockSpec((1, H2, L), lambda d, j: (d, 0, cidx(d, j))),
                  pl.BlockSpec((1, 1, H2), lambda d, j: (d, 0, 0)),
                  pl.BlockSpec((1, H2, 1), lambda d, j: (d, 0, 0))],
        out_specs=pl.BlockSpec((1, L, w), lambda d, j: (d, cidx(d, j), 0)),
        out_shape=jax.ShapeDtypeStruct((2, n, w), F32),
        scratch_shapes=[pltpu.VMEM((MLSTM_HEADS, MLSTM_HD, MLSTM_HD), F32),
                        pltpu.VMEM((MLSTM_HEADS, MLSTM_HD), F32),
                        pltpu.VMEM((MLSTM_HEADS, MLSTM_HD), F32)],
        compiler_params=_cparams(2),
        name="mlstm_scan",
    )(qp, kp, p, gcol, grow, bcol, brow)


def _na_kernel(q_ref, k_ref, v_ref, bias_ref, o_ref):
    t = pl.program_id(0)
    ctx_blocks = CTX_LEN // GRID_W
    r = jnp.maximum(t - ctx_blocks, 0)
    rs = jnp.clip(r - NA_WIN_H // 2, 0, GRID_ROWS - NA_WIN_H)
    nk = NA_WIN_H * GRID_W
    start = pl.multiple_of(CTX_LEN + rs * GRID_W, GRID_W)
    ks = k_ref[pl.ds(start, nk), :]
    vs = v_ref[pl.ds(start, nk), :]
    kc = k_ref[0:CTX_LEN, :]
    vc = v_ref[0:CTX_LEN, :]
    q = (q_ref[...] * np.float32(NA_HD ** -0.5)).astype(BF16)
    nt = (((1,), (1,)), ((), ()))
    for h in range(NA_HEADS):
        sl = slice(h * NA_HD, (h + 1) * NA_HD)
        qh = q[:, sl]
        s_loc = lax.dot_general(qh, ks[:, sl], nt, preferred_element_type=F32) + bias_ref[0, h]
        s_ctx = lax.dot_general(qh, kc[:, sl], nt, preferred_element_type=F32)
        m = jnp.maximum(jnp.max(s_loc, axis=-1, keepdims=True), jnp.max(s_ctx, axis=-1, keepdims=True))
        e_loc = jnp.exp(s_loc - m)
        e_ctx = jnp.exp(s_ctx - m)
        l = jnp.sum(e_loc, axis=-1, keepdims=True) + jnp.sum(e_ctx, axis=-1, keepdims=True)
        o = (jnp.dot(e_loc.astype(BF16), vs[:, sl], preferred_element_type=F32)
             + jnp.dot(e_ctx.astype(BF16), vc[:, sl], preferred_element_type=F32))
        o_ref[:, sl] = (o * (1.0 / l)).astype(o_ref.dtype)


def _na_bias_variant(t):
    ctx_blocks = CTX_LEN // GRID_W
    r = jnp.maximum(t - ctx_blocks, 0)
    rs = jnp.clip(r - NA_WIN_H // 2, 0, GRID_ROWS - NA_WIN_H)
    return jnp.where(t < ctx_blocks, NA_WIN_H, r - rs)


def _na_call(p, k_all, v_all, bias_tab):
    n = p.shape[0]
    q_col = (2 * SGU_WIDTH + 4 * MLSTM_WIDTH) // NA_WIDTH
    nk = NA_WIN_H * GRID_W
    return pl.pallas_call(
        _na_kernel,
        grid=(n // GRID_W,),
        in_specs=[pl.BlockSpec((GRID_W, NA_WIDTH), lambda t: (t, q_col)),
                  pl.BlockSpec((n, NA_WIDTH), lambda t: (0, 0)),
                  pl.BlockSpec((n, NA_WIDTH), lambda t: (0, 0)),
                  pl.BlockSpec((1, NA_HEADS, GRID_W, nk), lambda t: (_na_bias_variant(t), 0, 0, 0))],
        out_specs=pl.BlockSpec((GRID_W, NA_WIDTH), lambda t: (t, 0)),
        out_shape=jax.ShapeDtypeStruct((n, NA_WIDTH), BF16),
        compiler_params=_cparams(1),
        name="nbr_attn",
    )(p, k_all, v_all, bias_tab)


def _na_bias_table(rpb):
    cols = np.arange(GRID_W)
    col_start = np.clip(cols - NA_WIN_W // 2, 0, GRID_W - NA_WIN_W)
    kc = np.arange(GRID_W)
    in_win = (kc[None, :] >= col_start[:, None]) & (kc[None, :] < col_start[:, None] + NA_WIN_W)
    dc = np.clip(kc[None, :] - cols[:, None] + (NA_WIN_W - 1), 0, 2 * NA_WIN_W - 2)
    off = np.arange(NA_WIN_H)
    jj = np.arange(NA_WIN_H)
    dr = jj[None, :] - off[:, None] + (NA_WIN_H - 1)
    g = rpb[:, dr[:, None, :, None], dc[None, :, None, :]]
    g = jnp.where(in_win[None, None, :, None, :], g, NEG_BIG)
    g = g.reshape(NA_HEADS, NA_WIN_H, GRID_W, NA_WIN_H * GRID_W).transpose(1, 0, 2, 3)
    masked = jnp.full((1,) + g.shape[1:], NEG_BIG, F32)
    return jnp.concatenate([g, masked], axis=0)


def _outproj_kernel(h_ref, a_ref, hf_ref, hb_ref, og_ref, ng_ref, c_ref, w_ref, mod_ref, o_ref, *, tm):
    i = pl.program_id(0)
    is_ctx = _is_ctx_rows(i * tm, tm)
    parts = []
    for hd in range(MLSTM_HEADS):
        sl = slice(hd * MLSTM_HD, (hd + 1) * MLSTM_HD)
        hh = _rms(hf_ref[0, :, sl] + hb_ref[0, :, sl], ng_ref[:, sl])
        parts.append((_sigmoid(og_ref[:, sl]) * hh).astype(BF16))
    b = jnp.concatenate(parts, axis=-1)
    y = jnp.dot(a_ref[...], w_ref[0:SGU_WIDTH, :], preferred_element_type=F32)
    y = y + jnp.dot(b, w_ref[SGU_WIDTH:SGU_WIDTH + MLSTM_WIDTH, :], preferred_element_type=F32)
    y = y + jnp.dot(c_ref[...], w_ref[SGU_WIDTH + MLSTM_WIDTH:, :], preferred_element_type=F32)
    gate = jnp.where(is_ctx, mod_ref[5:6, :], mod_ref[2:3, :])
    o_ref[...] = h_ref[...] + gate * y


def _outproj_call(h, a, hdir, p, norm_g, c, w_out, mod):
    n, d = h.shape
    tm = TM_OUT
    w = MLSTM_WIDTH
    o_col = 4
    return pl.pallas_call(
        functools.partial(_outproj_kernel, tm=tm),
        grid=(n // tm,),
        in_specs=[pl.BlockSpec((tm, d), lambda i: (i, 0)),
                  pl.BlockSpec((tm, SGU_WIDTH), lambda i: (i, 0)),
                  pl.BlockSpec((1, tm, w), lambda i: (0, i, 0)),
                  pl.BlockSpec((1, tm, w), lambda i: (1, i, 0)),
                  pl.BlockSpec((tm, w), lambda i: (i, o_col)),
                  pl.BlockSpec((1, w), lambda i: (0, 0)),
                  pl.BlockSpec((tm, NA_WIDTH), lambda i: (i, 0)),
                  pl.BlockSpec((d, d), lambda i: (0, 0)),
                  pl.BlockSpec((8, d), lambda i: (0, 0))],
        out_specs=pl.BlockSpec((tm, d), lambda i: (i, 0)),
        out_shape=jax.ShapeDtypeStruct((n, d), F32),
        compiler_params=_cparams(1),
        name="mix_outproj",
    )(h, a, hdir, hdir, p, norm_g, c, w_out, mod)


def _final_norm_kernel(x_ref, g_ref, o_ref):
    o_ref[...] = _rms(x_ref[...], g_ref[...])


def _final_norm_call(h, g):
    n, d = h.shape
    tm = CTX_LEN
    return pl.pallas_call(
        _final_norm_kernel,
        grid=(SEQ // tm,),
        in_specs=[pl.BlockSpec((tm, d), lambda i: (i + 1, 0)),
                  pl.BlockSpec((1, d), lambda i: (0, 0))],
        out_specs=pl.BlockSpec((tm, d), lambda i: (i, 0)),
        out_shape=jax.ShapeDtypeStruct((SEQ, d), F32),
        compiler_params=_cparams(1),
        name="final_norm",
    )(h, g)


def _rope_tables():
    nf = MLSTM_HD // 4
    freqs = ROPE_THETA ** (-jnp.arange(nf, dtype=F32) / nf)
    t = jnp.arange(SEQ)
    row = (t // GRID_W).astype(F32)
    col = (t % GRID_W).astype(F32)
    ang_r = row[:, None] * freqs
    ang_c = col[:, None] * freqs
    cos = jnp.concatenate([jnp.cos(ang_r), jnp.cos(ang_r), jnp.cos(ang_c), jnp.cos(ang_c)], axis=-1)
    sin = jnp.concatenate([-jnp.sin(ang_r), jnp.sin(ang_r), -jnp.sin(ang_c), jnp.sin(ang_c)], axis=-1)
    cos = jnp.concatenate([jnp.ones((CTX_LEN, MLSTM_HD), F32), cos], axis=0)
    sin = jnp.concatenate([jnp.zeros((CTX_LEN, MLSTM_HD), F32), sin], axis=0)
    return cos, sin


def _mod_rows(mods_l, first):
    lat = mods_l[0, first:first + 3]
    ctx = mods_l[1, first:first + 3]
    return jnp.concatenate([lat, ctx, jnp.zeros((2, D_MODEL), F32)], axis=0)


def kernel(x, c, ctx, c_ctx, ada_w, ada_b, norm_g, ffn1_w13, ffn1_w2, ffn2_w13, ffn2_w2, mix_w_in, mix_w_out,
           sgu_norm_g, sgu_w, sgu_b, mlstm_conv_w, mlstm_conv_b, mlstm_gate_b, mlstm_norm_g, na_rpb,
           final_norm_g):
    assert x.shape == (1, SEQ, D_MODEL) and ctx.shape == (1, CTX_LEN, D_MODEL)
    h = jnp.concatenate([ctx[0], x[0]], axis=0)
    cc = jnp.concatenate([c, c_ctx[None, :], jnp.zeros((6, D_MODEL), F32)], axis=0)
    mods = _ada_call(cc, ada_w, ada_b)[:, :2].reshape(DEPTH, 2, N_MOD, D_MODEL)
    cos_t, sin_t = _rope_tables()
    H = MLSTM_HEADS
    gate_lo = 2 * SGU_WIDTH + 4 * MLSTM_WIDTH
    na_lo = gate_lo + 4 * H

    for l in range(DEPTH):
        ng = norm_g[l]
        h = _ffn_call(h, ng[0:1], _mod_rows(mods[l], 0), ffn1_w13[l].astype(BF16), ffn1_w2[l].astype(BF16))

        w_in = mix_w_in[l]
        w_main = jnp.concatenate([w_in[:, :gate_lo], w_in[:, na_lo:]], axis=1).astype(BF16)
        w_gate = jnp.pad(w_in[:, gate_lo:na_lo], ((0, 0), (0, GATE_PAD - 4 * H))).astype(BF16)
        mod_mix = _mod_rows(mods[l], 3)
        p, gates = _inproj_call(h, ng[1:2], mod_mix, w_main, w_gate)

        b_full = jnp.repeat(sgu_b[l].T, SGU_CH, axis=1)
        a = _sgu_call(p, sgu_norm_g[l][None, :], sgu_w[l], b_full)

        qp, kp = _mlstm_prep_call(p, mlstm_conv_w[l], mlstm_conv_b[l], cos_t, sin_t)
        gcol = gates[:, :4 * H].reshape(N_TOK, 2, 2 * H).transpose(1, 0, 2)
        grow = gcol.transpose(0, 2, 1)
        gb = mlstm_gate_b[l].reshape(2, 2 * H)
        hdir = _mlstm_scan_call(qp, kp, p, gcol, grow, gb[:, None, :], gb[:, :, None])

        k_all = p[:, gate_lo + NA_WIDTH:gate_lo + 2 * NA_WIDTH].astype(BF16)
        v_all = p[:, gate_lo + 2 * NA_WIDTH:gate_lo + 3 * NA_WIDTH].astype(BF16)
        cattn = _na_call(p, k_all, v_all, _na_bias_table(na_rpb[l]))

        h = _outproj_call(h, a, hdir, p, mlstm_norm_g[l][None, :], cattn, mix_w_out[l].astype(BF16), mod_mix)
        h = _ffn_call(h, ng[2:3], _mod_rows(mods[l], 6), ffn2_w13[l].astype(BF16), ffn2_w2[l].astype(BF16))

    return _final_norm_call(h, final_norm_g[None, :])[None]
```

```python
import functools

import numpy as np
import jax
import jax.numpy as jnp
from jax import lax
from jax.experimental import pallas as pl
from jax.experimental.pallas import tpu as pltpu

F32 = jnp.float32
BF16 = jnp.bfloat16

D_MODEL = 2048
SEQ = 8192
DEPTH = 4
GRID_W = 64
CTX_LEN = 256
N_TOK = CTX_LEN + SEQ
D_FF = 5632
N_MOD = 9
EPS = 1e-6
ROPE_THETA = 10000.0
SGU_WIDTH = 512
SGU_GROUPS = 4
SGU_CH = SGU_WIDTH // SGU_GROUPS
SGU_CHUNK = 128
NA_WIDTH = 512
NA_HEADS = 8
NA_HD = NA_WIDTH // NA_HEADS
NA_WIN_H = 8
NA_WIN_W = 16
MLSTM_WIDTH = 1024
MLSTM_HEADS = 8
MLSTM_HD = MLSTM_WIDTH // MLSTM_HEADS
MLSTM_CHUNK = 128
MLSTM_CONV_W = 5
GRID_ROWS = SEQ // GRID_W
LANES = 128
PM_COLS = 2 * SGU_WIDTH + 4 * MLSTM_WIDTH
PN_COLS = 3 * NA_WIDTH
GATE_COLS = 4 * MLSTM_HEADS
NEG_BIG = -1e30

VMEM_LIMIT = 60 * 1024 * 1024
TM_FFN = 528
TM_IN = 768
TM_OUT = 384
TF = 512
TN_IN = 512
TM_PREP = 256
HALO = 8
NA_ROWS = 4

NT_DIMS = (((1,), (1,)), ((), ()))


def _cparams(n_axes):
    return pltpu.CompilerParams(dimension_semantics=("arbitrary",) * n_axes,
                                vmem_limit_bytes=VMEM_LIMIT)


def _sigmoid(x):
    return 1.0 / (1.0 + jnp.exp(-x))


def _rms(x, g):
    return x * lax.rsqrt(jnp.mean(x * x, axis=-1, keepdims=True) + EPS) * g


def _is_ctx_rows(row0, tm):
    rows = row0 + lax.broadcasted_iota(jnp.int32, (tm, 1), 0)
    return rows < CTX_LEN


def _norm_modulate(x, g, mod_ref, is_ctx):
    y = _rms(x, g)
    shift = jnp.where(is_ctx, mod_ref[3:4, :], mod_ref[0:1, :])
    scale = jnp.where(is_ctx, mod_ref[4:5, :], mod_ref[1:2, :])
    return y * (1.0 + scale) + shift


def _mod_spec(l, s, grid_rank):
    zeros = (0,) * 2
    return pl.BlockSpec((None, None, 8, D_MODEL), lambda *_: (l, s) + zeros)


def _gain_spec(l, s):
    return pl.BlockSpec((None, None, 1, D_MODEL), lambda *_: (l, s, 0, 0))


def _ada_kernel(x_ref, w_ref, b_ref, o_ref):
    x = x_ref[...]
    xs = (x * _sigmoid(x)).astype(BF16)
    o_ref[0] = jnp.dot(xs, w_ref[0].astype(BF16), preferred_element_type=F32) + b_ref[0]


def _ada_call(cc, ada_w, ada_b):
    depth, d, nd = ada_w.shape
    tn = 1024
    return pl.pallas_call(
        _ada_kernel,
        grid=(depth, nd // tn),
        in_specs=[pl.BlockSpec((8, d), lambda l, j: (0, 0)),
                  pl.BlockSpec((1, d, tn), lambda l, j: (l, 0, j)),
                  pl.BlockSpec((1, 1, tn), lambda l, j: (l, 0, j))],
        out_specs=pl.BlockSpec((1, 8, tn), lambda l, j: (l, 0, j)),
        out_shape=jax.ShapeDtypeStruct((depth, 8, nd), F32),
        compiler_params=_cparams(2),
        name="ada_proj",
    )(cc, ada_w, ada_b.reshape(depth, 1, nd))


def _cast_kernel(x_ref, o_ref):
    o_ref[...] = x_ref[...].astype(o_ref.dtype)


def _cast_call(w, tr, tc):
    depth, r, c = w.shape
    spec = pl.BlockSpec((1, tr, tc), lambda l, i, j: (l, i, j))
    return pl.pallas_call(
        _cast_kernel,
        grid=(depth, r // tr, c // tc),
        in_specs=[spec],
        out_specs=spec,
        out_shape=jax.ShapeDtypeStruct(w.shape, BF16),
        compiler_params=_cparams(3),
        name="cast_w",
    )(w)


def _cast_win_kernel(a_ref, c_ref, o_ref, *, n_main):
    j = pl.program_id(1)

    @pl.when(j < n_main)
    def _():
        o_ref[...] = a_ref[...].astype(o_ref.dtype)

    @pl.when(j >= n_main)
    def _():
        o_ref[...] = c_ref[...].astype(o_ref.dtype)


def _cast_win_call(w_in, w_na):
    depth, d, _ = w_in.shape
    tn = TN_IN
    n_main = PM_COLS // tn
    n_na = PN_COLS // tn
    return pl.pallas_call(
        functools.partial(_cast_win_kernel, n_main=n_main),
        grid=(depth, n_main + n_na),
        in_specs=[pl.BlockSpec((1, d, tn), lambda l, j: (l, 0, jnp.minimum(j, n_main - 1))),
                  pl.BlockSpec((1, d, tn), lambda l, j: (l, 0, jnp.maximum(j - n_main, 0)))],
        out_specs=pl.BlockSpec((1, d, tn), lambda l, j: (l, 0, j)),
        out_shape=jax.ShapeDtypeStruct((depth, d, PM_COLS + PN_COLS), BF16),
        compiler_params=_cparams(2),
        name="cast_w_in",
    )(w_in, w_na)


def _ffn_kernel(x_ref, g_ref, mod_ref, w1_ref, w3_ref, w2_ref, o_ref, xn_ref, *, tm, nk):
    i = pl.program_id(0)
    k = pl.program_id(1)
    is_ctx = _is_ctx_rows(i * tm, tm)

    @pl.when(k == 0)
    def _():
        xn_ref[...] = _norm_modulate(x_ref[...], g_ref[...], mod_ref, is_ctx).astype(BF16)

    xn = xn_ref[...]
    a = jnp.dot(xn, w1_ref[...], preferred_element_type=F32)
    g = jnp.dot(xn, w3_ref[...], preferred_element_type=F32)
    act = (a * _sigmoid(a) * g).astype(BF16)
    part = jnp.dot(act, w2_ref[...], preferred_element_type=F32)

    @pl.when(k == 0)
    def _():
        o_ref[...] = part

    @pl.when(k > 0)
    def _():
        o_ref[...] += part

    @pl.when(k == nk - 1)
    def _():
        gate = jnp.where(is_ctx, mod_ref[5:6, :], mod_ref[2:3, :])
        o_ref[...] = x_ref[...] + 0.5 * gate * o_ref[...]


def _ffn_call(h, gains, mods, w13, w2, l, s):
    n, d = h.shape
    nk = D_FF // TF
    tm = TM_FFN
    return pl.pallas_call(
        functools.partial(_ffn_kernel, tm=tm, nk=nk),
        grid=(n // tm, nk),
        in_specs=[pl.BlockSpec((tm, d), lambda i, k: (i, 0)),
                  _gain_spec(l, s),
                  _mod_spec(l, s, 2),
                  pl.BlockSpec((None, d, TF), lambda i, k: (l, 0, k)),
                  pl.BlockSpec((None, d, TF), lambda i, k: (l, 0, k + nk)),
                  pl.BlockSpec((None, TF, d), lambda i, k: (l, k, 0))],
        out_specs=pl.BlockSpec((tm, d), lambda i, k: (i, 0)),
        out_shape=jax.ShapeDtypeStruct((n, d), F32),
        scratch_shapes=[pltpu.VMEM((tm, d), BF16)],
        compiler_params=_cparams(2),
        name="ffn",
    )(h, gains, mods, w13, w13, w2)


def _inproj_kernel(x_ref, g_ref, mod_ref, w_ref, wg_ref, pm_ref, pn_ref, gt_ref, xn_ref, *, tm, n_main):
    i = pl.program_id(0)
    j = pl.program_id(1)

    @pl.when(j == 0)
    def _():
        is_ctx = _is_ctx_rows(i * tm, tm)
        xn = _norm_modulate(x_ref[...], g_ref[...], mod_ref, is_ctx).astype(BF16)
        xn_ref[...] = xn
        gt_ref[...] = lax.dot_general(wg_ref[...], xn, NT_DIMS, preferred_element_type=F32)

    res = jnp.dot(xn_ref[...], w_ref[...], preferred_element_type=F32)

    @pl.when(j < n_main)
    def _():
        pm_ref[...] = res

    @pl.when(j >= n_main)
    def _():
        pn_ref[...] = res.astype(pn_ref.dtype)


def _inproj_call(h, gains, mods, w_all, w_gate, l):
    n, d = h.shape
    tm = TM_IN
    tn = TN_IN
    n_main = PM_COLS // tn
    n_na = PN_COLS // tn
    return pl.pallas_call(
        functools.partial(_inproj_kernel, tm=tm, n_main=n_main),
        grid=(n // tm, n_main + n_na),
        in_specs=[pl.BlockSpec((tm, d), lambda i, j: (i, 0)),
                  _gain_spec(l, 1),
                  _mod_spec(l, 1, 2),
                  pl.BlockSpec((None, d, tn), lambda i, j: (l, 0, j)),
                  pl.BlockSpec((None, LANES, d), lambda i, j: (l, 0, 0))],
        out_specs=[pl.BlockSpec((tm, tn), lambda i, j: (i, jnp.minimum(j, n_main - 1))),
                   pl.BlockSpec((tm, tn), lambda i, j: (i, jnp.maximum(j - n_main, 0))),
                   pl.BlockSpec((LANES, tm), lambda i, j: (0, i))],
        out_shape=[jax.ShapeDtypeStruct((n, PM_COLS), F32),
                   jax.ShapeDtypeStruct((n, PN_COLS), BF16),
                   jax.ShapeDtypeStruct((LANES, n), F32)],
        scratch_shapes=[pltpu.VMEM((tm, d), BF16)],
        compiler_params=_cparams(2),
        name="mix_inproj",
    )(h, gains, mods, w_all, w_gate)


def _gelu_tanh(x):
    c = np.float32(np.sqrt(2.0 / np.pi))
    return x * (0.5 * (1.0 + jnp.tanh(c * (x + 0.044715 * (x * x * x)))))


def _sgu_kernel(u_ref, v_ref, g_ref, w_ref, b_ref, o_ref):
    u = _gelu_tanh(u_ref[...])
    v = _rms(_gelu_tanh(v_ref[...]), g_ref[...]).astype(BF16)
    for grp in range(SGU_GROUPS):
        sl = slice(grp * SGU_CH, (grp + 1) * SGU_CH)
        mixed = jnp.dot(w_ref[grp].astype(BF16), v[:, sl], preferred_element_type=F32) + b_ref[:, sl]
        o_ref[:, sl] = (u[:, sl] * mixed).astype(o_ref.dtype)


def _sgu_call(pm, norm_g, w_s, b_full, l):
    n = pm.shape[0]
    c = SGU_CHUNK
    return pl.pallas_call(
        _sgu_kernel,
        grid=(n // c,),
        in_specs=[pl.BlockSpec((c, SGU_WIDTH), lambda i: (i, 0)),
                  pl.BlockSpec((c, SGU_WIDTH), lambda i: (i, 1)),
                  pl.BlockSpec((None, 1, SGU_WIDTH), lambda i: (l, 0, 0)),
                  pl.BlockSpec((None, SGU_GROUPS, c, c), lambda i: (l, 0, 0, 0)),
                  pl.BlockSpec((None, c, SGU_WIDTH), lambda i: (l, 0, 0))],
        out_specs=pl.BlockSpec((c, SGU_WIDTH), lambda i: (i, 0)),
        out_shape=jax.ShapeDtypeStruct((n, SGU_WIDTH), BF16),
        compiler_params=_cparams(1),
        name="sgu",
    )(pm, pm, norm_g, w_s, b_full)


def _prep_one(x_ref, prev_ref, next_ref, w_ref, b_ref, cos, sin, ext_ref, o_ref, zero_prev, zero_next,
              out_scale, tm, transposed):
    ext_ref[0:HALO, :] = jnp.where(zero_prev, 0.0, prev_ref[...])
    ext_ref[HALO:HALO + tm, :] = x_ref[...]
    ext_ref[HALO + tm:, :] = jnp.where(zero_next, 0.0, next_ref[...])
    half = MLSTM_CONV_W // 2
    y = b_ref[...] + w_ref[0:1, :] * ext_ref[HALO - half:HALO - half + tm, :]
    for j in range(1, MLSTM_CONV_W):
        y = y + w_ref[j:j + 1, :] * ext_ref[HALO - half + j:HALO - half + j + tm, :]
    y = y * _sigmoid(y)
    lane = lax.broadcasted_iota(jnp.int32, (tm, MLSTM_HD), 1)
    first = (lane % (MLSTM_HD // 2)) < (MLSTM_HD // 4)
    for hd in range(MLSTM_HEADS):
        sl = slice(hd * MLSTM_HD, (hd + 1) * MLSTM_HD)
        yh = y[:, sl]
        partner = jnp.where(first, pltpu.roll(yh, MLSTM_HD - MLSTM_HD // 4, 1),
                            pltpu.roll(yh, MLSTM_HD // 4, 1))
        val = (yh * cos + partner * sin) * out_scale
        if transposed:
            o_ref[sl, :] = val.T.astype(o_ref.dtype)
        else:
            o_ref[:, sl] = val.astype(o_ref.dtype)


def _mlstm_prep_kernel(q_ref, qp_ref, qn_ref, k_ref, kp_ref, kn_ref, wq_ref, wk_ref, bq_ref, bk_ref,
                       cos_ref, sin_ref, qo_ref, kt_ref, ext_ref, *, tm, n_tiles):
    i = pl.program_id(0)
    row0 = i * tm
    zero_prev = jnp.logical_or(row0 == 0, row0 == CTX_LEN)
    zero_next = jnp.logical_or(row0 + tm == CTX_LEN, i == n_tiles - 1)
    cos = cos_ref[...]
    sin = sin_ref[...]
    _prep_one(q_ref, qp_ref, qn_ref, wq_ref, bq_ref, cos, sin, ext_ref, qo_ref, zero_prev, zero_next,
              np.float32(MLSTM_HD ** -0.5), tm, False)
    _prep_one(k_ref, kp_ref, kn_ref, wk_ref, bk_ref, cos, sin, ext_ref, kt_ref, zero_prev, zero_next,
              np.float32(1.0), tm, True)


def _mlstm_prep_call(pm, conv_w, conv_b, cos_t, sin_t, l):
    n = pm.shape[0]
    tm = TM_PREP
    n_tiles = n // tm
    w = MLSTM_WIDTH
    hb = tm // HALO
    last_hb = n // HALO - 1
    q_col, k_col = 1, 2
    main = lambda col: pl.BlockSpec((tm, w), lambda i: (i, col))
    prev = lambda col: pl.BlockSpec((HALO, w), lambda i: (jnp.maximum(i * hb - 1, 0), col))
    nxt = lambda col: pl.BlockSpec((HALO, w), lambda i: (jnp.minimum((i + 1) * hb, last_hb), col))
    wspec = lambda col: pl.BlockSpec((None, MLSTM_CONV_W, w), lambda i: (l, 0, col))
    bspec = lambda col: pl.BlockSpec((None, 1, w), lambda i: (l, 0, col))
    tab = pl.BlockSpec((tm, MLSTM_HD), lambda i: (i, 0))
    return pl.pallas_call(
        functools.partial(_mlstm_prep_kernel, tm=tm, n_tiles=n_tiles),
        grid=(n_tiles,),
        in_specs=[main(q_col), prev(q_col), nxt(q_col), main(k_col), prev(k_col), nxt(k_col),
                  wspec(0), wspec(1), bspec(0), bspec(1), tab, tab],
        out_specs=[pl.BlockSpec((tm, w), lambda i: (i, 0)), pl.BlockSpec((w, tm), lambda i: (0, i))],
        out_shape=[jax.ShapeDtypeStruct((n, w), BF16), jax.ShapeDtypeStruct((w, n), BF16)],
        scratch_shapes=[pltpu.VMEM((tm + 2 * HALO, w), F32)],
        compiler_params=_cparams(1),
        name="mlstm_prep",
    )(pm, pm, pm, pm, pm, pm, conv_w, conv_w, conv_b, conv_b, cos_t, sin_t)


def _log_sigmoid(x):
    return jnp.minimum(x, 0.0) - jnp.log(1.0 + jnp.exp(-jnp.abs(x)))


def _split3(x):
    hi = x.astype(BF16)
    r1 = x - hi.astype(F32)
    mid = r1.astype(BF16)
    lo = (r1 - mid.astype(F32)).astype(BF16)
    return hi, mid, lo


def _mlstm_chunk(dd, q_ref, kt_ref, v_ref, gr, o_ref, c_ref, n_ref, m_ref):
    L = MLSTM_CHUNK
    H = MLSTM_HEADS
    hd = MLSTM_HD
    r_idx = lax.broadcasted_iota(jnp.int32, (L, L), 0)
    c_idx = lax.broadcasted_iota(jnp.int32, (L, L), 1)
    tri = (c_idx <= r_idx) if dd == 0 else (c_idx >= r_idx)
    tri_t = (c_idx >= r_idx) if dd == 0 else (c_idx <= r_idx)
    tri_b = jnp.where(tri, 1.0, 0.0).astype(BF16)
    tri_t_b = jnp.where(tri_t, 1.0, 0.0).astype(BF16)
    tri3 = jnp.concatenate([tri_b, tri_b, tri_b], axis=1)

    base = 2 * H * dd
    i_row = gr[base:base + H, :]
    lf_all = _log_sigmoid(gr[base:base + 2 * H, :])
    lf_row = lf_all[H:2 * H, :]
    parts = _split3(lf_all)
    parts_f = [p.astype(F32)[H:2 * H, :] for p in parts]
    b_row = sum(jnp.dot(p, tri_t_b, preferred_element_type=F32) for p in parts)[H:2 * H, :]
    b_last = jnp.sum(lf_row, axis=-1, keepdims=True)
    ib_row = i_row - b_row
    g_row = b_last - b_row + i_row
    g_max = jnp.max(g_row, axis=-1, keepdims=True)

    for h in range(H):
        sl = slice(h * hd, (h + 1) * hd)
        q = q_ref[:, sl]
        kt = kt_ref[sl, :]
        v = v_ref[:, sl].astype(BF16)
        m_old = m_ref[dd, h:h + 1, :]
        c_old = c_ref[dd, h]
        n_old = n_ref[dd, h]
        rows3 = jnp.concatenate([jnp.broadcast_to(p[h:h + 1, :], (LANES, L)) for p in parts_f],
                                axis=1).astype(BF16)
        b_col = lax.dot_general(tri3, rows3, NT_DIMS, preferred_element_type=F32)
        d_mat = jnp.where(tri, b_col + ib_row[h:h + 1, :], -jnp.inf)
        inter = b_col + m_old
        m_t = jnp.maximum(jnp.max(d_mat, axis=-1, keepdims=True), inter)
        qk = jnp.dot(q, kt, preferred_element_type=F32)
        s = qk * jnp.exp(d_mat - m_t)
        a = jnp.exp(inter - m_t)
        qc = jnp.dot(q, c_old.astype(BF16), preferred_element_type=F32)
        num = a * qc + jnp.dot(s.astype(BF16), v, preferred_element_type=F32)
        qn = jnp.dot(q, n_old.astype(BF16), preferred_element_type=F32)
        den = a * qn + jnp.sum(s, axis=-1, keepdims=True)
        o_ref[:, sl] = num * (1.0 / jnp.maximum(jnp.abs(den), jnp.exp(-m_t)))
        bl = b_last[h:h + 1, :]
        m_new = jnp.maximum(bl + m_old, g_max[h:h + 1, :])
        w_row = jnp.exp(g_row[h:h + 1, :] - m_new)
        decay = jnp.exp(bl + m_old - m_new)
        kw = kt.astype(F32) * w_row
        c_ref[dd, h] = decay * c_old + jnp.dot(kw.astype(BF16), v, preferred_element_type=F32)
        n_ref[dd, h] = decay * n_old + jnp.sum(kw, axis=-1, keepdims=True)
        m_ref[dd, h:h + 1, :] = m_new


def _mlstm_scan_kernel(qf_ref, ktf_ref, vf_ref, gf_ref, qb_ref, ktb_ref, vb_ref, gb_ref, bias_ref,
                       of_ref, ob_ref, c_ref, n_ref, m_ref):
    @pl.when(pl.program_id(0) == 0)
    def _():
        c_ref[...] = jnp.zeros_like(c_ref)
        n_ref[...] = jnp.zeros_like(n_ref)
        m_ref[...] = jnp.zeros_like(m_ref)

    bias = bias_ref[...]
    _mlstm_chunk(0, qf_ref, ktf_ref, vf_ref, gf_ref[...] + bias, of_ref, c_ref, n_ref, m_ref)
    _mlstm_chunk(1, qb_ref, ktb_ref, vb_ref, gb_ref[...] + bias, ob_ref, c_ref, n_ref, m_ref)


def _mlstm_bwd_chunk(j, n_chunks):
    ctx_chunks = CTX_LEN // MLSTM_CHUNK
    return jnp.where(j < ctx_chunks, ctx_chunks - 1 - j, n_chunks + ctx_chunks - 1 - j)


def _mlstm_scan_call(qp, kt, pm, gates_t, gate_bias, l):
    n = qp.shape[0]
    L = MLSTM_CHUNK
    nc = n // L
    w = MLSTM_WIDTH
    v_col = 3
    fwd = lambda j: j
    bwd = lambda j: _mlstm_bwd_chunk(j, nc)

    def specs(cidx):
        return [pl.BlockSpec((L, w), lambda j: (cidx(j), 0)),
                pl.BlockSpec((w, L), lambda j: (0, cidx(j))),
                pl.BlockSpec((L, w), lambda j: (cidx(j), v_col)),
                pl.BlockSpec((LANES, L), lambda j: (0, cidx(j)))]

    state = lambda last: pltpu.VMEM((2, MLSTM_HEADS, MLSTM_HD, last), F32)
    return pl.pallas_call(
        _mlstm_scan_kernel,
        grid=(nc,),
        in_specs=specs(fwd) + specs(bwd) + [pl.BlockSpec((None, LANES, 1), lambda j: (l, 0, 0))],
        out_specs=[pl.BlockSpec((L, w), lambda j: (fwd(j), 0)),
                   pl.BlockSpec((L, w), lambda j: (bwd(j), 0))],
        out_shape=[jax.ShapeDtypeStruct((n, w), F32), jax.ShapeDtypeStruct((n, w), F32)],
        scratch_shapes=[state(MLSTM_HD), state(LANES), pltpu.VMEM((2, MLSTM_HEADS, LANES), F32)],
        compiler_params=_cparams(1),
        name="mlstm_scan",
    )(qp, kt, pm, gates_t, qp, kt, pm, gates_t, gate_bias)


def _na_row_geometry(tt):
    ctx_blocks = CTX_LEN // GRID_W
    r = jnp.maximum(tt - ctx_blocks, 0)
    rs = jnp.clip(r - NA_WIN_H // 2, 0, GRID_ROWS - NA_WIN_H)
    return rs, jnp.where(tt < ctx_blocks, NA_WIN_H, r - rs)


def _na_kernel(q_ref, k_ref, v_ref, *rest):
    bias_refs, o_ref = rest[:NA_ROWS], rest[NA_ROWS]
    t = pl.program_id(0)
    nk = NA_WIN_H * GRID_W
    kc = k_ref[0:CTX_LEN, :]
    vc = v_ref[0:CTX_LEN, :]
    lo = lax.broadcasted_iota(jnp.int32, (GRID_W, LANES), 1) < NA_HD
    for rr in range(NA_ROWS):
        rs, _ = _na_row_geometry(t * NA_ROWS + rr)
        start = pl.multiple_of(CTX_LEN + rs * GRID_W, GRID_W)
        ks = k_ref[pl.ds(start, nk), :]
        vs = v_ref[pl.ds(start, nk), :]
        q = q_ref[rr * GRID_W:(rr + 1) * GRID_W, :] * (NA_HD ** -0.5)
        for pr in range(NA_HEADS // 2):
            sl = slice(pr * LANES, (pr + 1) * LANES)
            qp = q[:, sl]
            zero = jnp.zeros_like(qp)
            qs = jnp.concatenate([jnp.where(lo, qp, zero), jnp.where(lo, zero, qp)], axis=0)
            s_loc = lax.dot_general(qs, ks[:, sl], NT_DIMS, preferred_element_type=F32) + bias_refs[rr][0, pr]
            s_ctx = lax.dot_general(qs, kc[:, sl], NT_DIMS, preferred_element_type=F32)
            m = jnp.maximum(jnp.max(s_loc, axis=-1, keepdims=True), jnp.max(s_ctx, axis=-1, keepdims=True))
            e_loc = jnp.exp(s_loc - m)
            e_ctx = jnp.exp(s_ctx - m)
            den = jnp.sum(e_loc, axis=-1, keepdims=True) + jnp.sum(e_ctx, axis=-1, keepdims=True)
            o = (jnp.dot(e_loc.astype(BF16), vs[:, sl], preferred_element_type=F32)
                 + jnp.dot(e_ctx.astype(BF16), vc[:, sl], preferred_element_type=F32)) * (1.0 / den)
            o_ref[rr * GRID_W:(rr + 1) * GRID_W, sl] = jnp.where(lo, o[0:GRID_W], o[GRID_W:]).astype(o_ref.dtype)


def _na_call(pn, bias_tab, l):
    n = pn.shape[0]
    nk = NA_WIN_H * GRID_W
    tq = NA_ROWS * GRID_W

    def bias_spec(rr):
        return pl.BlockSpec((None, 1, NA_HEADS // 2, 2 * GRID_W, nk),
                            lambda t: (l, _na_row_geometry(t * NA_ROWS + rr)[1], 0, 0, 0))

    return pl.pallas_call(
        _na_kernel,
        grid=(n // tq,),
        in_specs=[pl.BlockSpec((tq, NA_WIDTH), lambda t: (t, 0)),
                  pl.BlockSpec((n, NA_WIDTH), lambda t: (0, 1)),
                  pl.BlockSpec((n, NA_WIDTH), lambda t: (0, 2))] + [bias_spec(rr) for rr in range(NA_ROWS)],
        out_specs=pl.BlockSpec((tq, NA_WIDTH), lambda t: (t, 0)),
        out_shape=jax.ShapeDtypeStruct((n, NA_WIDTH), BF16),
        compiler_params=_cparams(1),
        name="nbr_attn",
    )(pn, pn, pn, *([bias_tab] * NA_ROWS))


def _na_bias_tables(rpb_all):
    depth = rpb_all.shape[0]
    cols = np.arange(GRID_W)
    col_start = np.clip(cols - NA_WIN_W // 2, 0, GRID_W - NA_WIN_W)
    in_win = (cols[None, :] >= col_start[:, None]) & (cols[None, :] < col_start[:, None] + NA_WIN_W)
    dc = cols[None, :] - cols[:, None] + (NA_WIN_W - 1)
    onehot = np.zeros((GRID_W, GRID_W, 2 * NA_WIN_W - 1), np.float32)
    cc, kk = np.nonzero(in_win)
    onehot[cc, kk, dc[cc, kk]] = 1.0
    mask_bias = np.where(in_win, 0.0, NEG_BIG).astype(np.float32)
    n_dr = 2 * NA_WIN_H - 1
    e = jnp.einsum('lhrd,ckd->lhcrk', rpb_all, jnp.asarray(onehot), precision=lax.Precision.HIGHEST)
    e = (e + mask_bias[None, None, :, None, :]).reshape(depth, NA_HEADS, GRID_W, n_dr * GRID_W)
    nk = NA_WIN_H * GRID_W
    variants = [e[..., (NA_WIN_H - 1 - off) * GRID_W:(NA_WIN_H - 1 - off) * GRID_W + nk] for off in range(NA_WIN_H)]
    variants.append(jnp.full_like(variants[0], NEG_BIG))
    tab = jnp.stack(variants, axis=1)
    return tab.reshape(depth, NA_WIN_H + 1, NA_HEADS // 2, 2 * GRID_W, nk)


def _outproj_kernel(h_ref, a_ref, hf_ref, hb_ref, og_ref, ng_ref, c_ref, w_ref, mod_ref, o_ref, *, tm):
    i = pl.program_id(0)
    is_ctx = _is_ctx_rows(i * tm, tm)
    parts = []
    for hd in range(MLSTM_HEADS):
        sl = slice(hd * MLSTM_HD, (hd + 1) * MLSTM_HD)
        hh = _rms(hf_ref[:, sl] + hb_ref[:, sl], ng_ref[:, sl])
        parts.append((_sigmoid(og_ref[:, sl]) * hh).astype(BF16))
    b = jnp.concatenate(parts, axis=-1)
    y = jnp.dot(a_ref[...], w_ref[0:SGU_WIDTH, :], preferred_element_type=F32)
    y = y + jnp.dot(b, w_ref[SGU_WIDTH:SGU_WIDTH + MLSTM_WIDTH, :], preferred_element_type=F32)
    y = y + jnp.dot(c_ref[...], w_ref[SGU_WIDTH + MLSTM_WIDTH:, :], preferred_element_type=F32)
    gate = jnp.where(is_ctx, mod_ref[5:6, :], mod_ref[2:3, :])
    o_ref[...] = h_ref[...] + gate * y


def _outproj_call(h, a, hf, hb, pm, norm_g, c, w_out, mods, l):
    n, d = h.shape
    tm = TM_OUT
    w = MLSTM_WIDTH
    o_col = 4
    return pl.pallas_call(
        functools.partial(_outproj_kernel, tm=tm),
        grid=(n // tm,),
        in_specs=[pl.BlockSpec((tm, d), lambda i: (i, 0)),
                  pl.BlockSpec((tm, SGU_WIDTH), lambda i: (i, 0)),
                  pl.BlockSpec((tm, w), lambda i: (i, 0)),
                  pl.BlockSpec((tm, w), lambda i: (i, 0)),
                  pl.BlockSpec((tm, w), lambda i: (i, o_col)),
                  pl.BlockSpec((None, 1, w), lambda i: (l, 0, 0)),
                  pl.BlockSpec((tm, NA_WIDTH), lambda i: (i, 0)),
                  pl.BlockSpec((None, d, d), lambda i: (l, 0, 0)),
                  _mod_spec(l, 1, 1)],
        out_specs=pl.BlockSpec((tm, d), lambda i: (i, 0)),
        out_shape=jax.ShapeDtypeStruct((n, d), F32),
        compiler_params=_cparams(1),
        name="mix_outproj",
    )(h, a, hf, hb, pm, norm_g, c, w_out, mods)


def _final_norm_kernel(x_ref, g_ref, o_ref):
    o_ref[...] = _rms(x_ref[...], g_ref[...])


def _final_norm_call(h, g):
    n, d = h.shape
    tm = CTX_LEN
    return pl.pallas_call(
        _final_norm_kernel,
        grid=(SEQ // tm,),
        in_specs=[pl.BlockSpec((tm, d), lambda i: (i + 1, 0)),
                  pl.BlockSpec((1, d), lambda i: (0, 0))],
        out_specs=pl.BlockSpec((tm, d), lambda i: (i, 0)),
        out_shape=jax.ShapeDtypeStruct((SEQ, d), F32),
        compiler_params=_cparams(1),
        name="final_norm",
    )(h, g)


def _rope_tables():
    nf = MLSTM_HD // 4
    freqs = ROPE_THETA ** (-jnp.arange(nf, dtype=F32) / nf)
    t = jnp.arange(SEQ)
    row = (t // GRID_W).astype(F32)
    col = (t % GRID_W).astype(F32)
    ang_r = row[:, None] * freqs
    ang_c = col[:, None] * freqs
    cos = jnp.concatenate([jnp.cos(ang_r), jnp.cos(ang_r), jnp.cos(ang_c), jnp.cos(ang_c)], axis=-1)
    sin = jnp.concatenate([-jnp.sin(ang_r), jnp.sin(ang_r), -jnp.sin(ang_c), jnp.sin(ang_c)], axis=-1)
    cos = jnp.concatenate([jnp.ones((CTX_LEN, MLSTM_HD), F32), cos], axis=0)
    sin = jnp.concatenate([jnp.zeros((CTX_LEN, MLSTM_HD), F32), sin], axis=0)
    return cos, sin


def _mod_table(mods):
    m = mods.reshape(DEPTH, 2, 3, 3, D_MODEL).transpose(0, 2, 1, 3, 4).reshape(DEPTH, 3, 6, D_MODEL)
    return jnp.pad(m, ((0, 0), (0, 0), (0, 2), (0, 0)))


def kernel(x, c, ctx, c_ctx, ada_w, ada_b, norm_g, ffn1_w13, ffn1_w2, ffn2_w13, ffn2_w2, mix_w_in, mix_w_out,
           sgu_norm_g, sgu_w, sgu_b, mlstm_conv_w, mlstm_conv_b, mlstm_gate_b, mlstm_norm_g, na_rpb,
           final_norm_g):
    assert x.shape == (1, SEQ, D_MODEL) and ctx.shape == (1, CTX_LEN, D_MODEL)
    h = jnp.concatenate([ctx[0], x[0]], axis=0)
    cc = jnp.concatenate([c, c_ctx[None, :], jnp.zeros((6, D_MODEL), F32)], axis=0)
    mods = _mod_table(_ada_call(cc, ada_w, ada_b)[:, :2].reshape(DEPTH, 2, N_MOD, D_MODEL))
    gains = norm_g[:, :, None, :]
    cos_t, sin_t = _rope_tables()

    w13_1 = _cast_call(ffn1_w13, D_MODEL, 1024)
    w13_2 = _cast_call(ffn2_w13, D_MODEL, 1024)
    w2_1 = _cast_call(ffn1_w2, D_FF // 4, D_MODEL)
    w2_2 = _cast_call(ffn2_w2, D_FF // 4, D_MODEL)
    w_out = _cast_call(mix_w_out, D_MODEL // 2, D_MODEL)
    w_all = _cast_win_call(mix_w_in, mix_w_in[:, :, PM_COLS + GATE_COLS:])
    w_gate = jnp.pad(mix_w_in[:, :, PM_COLS:PM_COLS + GATE_COLS].transpose(0, 2, 1),
                     ((0, 0), (0, LANES - GATE_COLS), (0, 0))).astype(BF16)
    gate_bias = jnp.pad(mlstm_gate_b.reshape(DEPTH, GATE_COLS), ((0, 0), (0, LANES - GATE_COLS)))[:, :, None]
    sgu_bias = jnp.repeat(sgu_b.transpose(0, 2, 1), SGU_CH, axis=2)
    bias_tab = _na_bias_tables(na_rpb)
    conv_b = mlstm_conv_b[:, None, :]

    for l in range(DEPTH):
        h = _ffn_call(h, gains, mods, w13_1, w2_1, l, 0)
        pm, pn, gates_t = _inproj_call(h, gains, mods, w_all, w_gate, l)
        a = _sgu_call(pm, sgu_norm_g[:, None, :], sgu_w, sgu_bias, l)
        qp, kt = _mlstm_prep_call(pm, mlstm_conv_w, conv_b, cos_t, sin_t, l)
        hf, hb = _mlstm_scan_call(qp, kt, pm, gates_t, gate_bias, l)
        cattn = _na_call(pn, bias_tab, l)
        h = _outproj_call(h, a, hf, hb, pm, mlstm_norm_g[:, None, :], cattn, w_out, mods, l)
        h = _ffn_call(h, gains, mods, w13_2, w2_2, l, 2)

    return _final_norm_call(h, final_norm_g[None, :])[None]
```

```python
import functools

import numpy as np
import jax
import jax.numpy as jnp
from jax import lax
from jax.experimental import pallas as pl
from jax.experimental.pallas import tpu as pltpu

F32 = jnp.float32
BF16 = jnp.bfloat16

D_MODEL = 2048
SEQ = 8192
DEPTH = 4
GRID_W = 64
CTX_LEN = 256
N_TOK = CTX_LEN + SEQ
D_FF = 5632
N_MOD = 9
EPS = 1e-6
ROPE_THETA = 10000.0
SGU_WIDTH = 512
SGU_GROUPS = 4
SGU_CH = SGU_WIDTH // SGU_GROUPS
SGU_CHUNK = 128
NA_WIDTH = 512
NA_HEADS = 8
NA_HD = NA_WIDTH // NA_HEADS
NA_WIN_H = 8
NA_WIN_W = 16
MLSTM_WIDTH = 1024
MLSTM_HEADS = 8
MLSTM_HD = MLSTM_WIDTH // MLSTM_HEADS
MLSTM_CHUNK = 128
MLSTM_CONV_W = 5
GRID_ROWS = SEQ // GRID_W
LANES = 128
PM_COLS = 2 * SGU_WIDTH + 4 * MLSTM_WIDTH
PN_COLS = 3 * NA_WIDTH
GATE_COLS = 4 * MLSTM_HEADS
NEG_BIG = -1e30

VMEM_LIMIT = 60 * 1024 * 1024
TM_FFN = 528
TM_IN = 768
TM_OUT = 384
TF = 512
TN_IN = 512
TM_PREP = 256
HALO = 8
NORM_ROWS = 16
SCAN_CHUNKS = 2
NA_ROWS = 4

NT_DIMS = (((1,), (1,)), ((), ()))


def _cparams(n_axes):
    return pltpu.CompilerParams(dimension_semantics=("arbitrary",) * n_axes,
                                vmem_limit_bytes=VMEM_LIMIT)


def _sigmoid(x):
    return 1.0 / (1.0 + jnp.exp(-x))


def _rms(x, g):
    return x * lax.rsqrt(jnp.mean(x * x, axis=-1, keepdims=True) + EPS) * g


def _is_ctx_rows(row0, tm):
    rows = row0 + lax.broadcasted_iota(jnp.int32, (tm, 1), 0)
    return rows < CTX_LEN


def _norm_modulate_store(x_ref, g_ref, mod_ref, xn_ref, rstd_ref, row0, tm):
    def stats(r, carry):
        rs = pl.multiple_of(r * 8, 8)
        x = x_ref[pl.ds(rs, 8), :]
        rstd = lax.rsqrt(jnp.mean(x * x, axis=-1, keepdims=True) + EPS)
        rstd_ref[pl.ds(rs, 8), :] = jnp.broadcast_to(rstd, (8, LANES))
        return carry

    lax.fori_loop(0, tm // 8, stats, 0, unroll=tm // 48)

    g = g_ref[...]
    lat_scale = g * (1.0 + mod_ref[1:2, :])
    ctx_scale = g * (1.0 + mod_ref[4:5, :])
    lat_shift = mod_ref[0:1, :]
    ctx_shift = mod_ref[3:4, :]

    def strip(r, carry):
        rs = pl.multiple_of(r * NORM_ROWS, NORM_ROWS)
        x = x_ref[pl.ds(rs, NORM_ROWS), :]
        rstd = rstd_ref[pl.ds(rs, NORM_ROWS), 0:1]
        is_ctx = row0 + rs < CTX_LEN
        scale = jnp.where(is_ctx, ctx_scale, lat_scale)
        shift = jnp.where(is_ctx, ctx_shift, lat_shift)
        xn_ref[pl.ds(rs, NORM_ROWS), :] = ((x * rstd) * scale + shift).astype(xn_ref.dtype)
        return carry

    lax.fori_loop(0, tm // NORM_ROWS, strip, 0, unroll=3)


def _mod_spec(l, s, grid_rank):
    zeros = (0,) * 2
    return pl.BlockSpec((None, None, 8, D_MODEL), lambda *_: (l, s) + zeros)


def _gain_spec(l, s):
    return pl.BlockSpec((None, None, 1, D_MODEL), lambda *_: (l, s, 0, 0))


def _ada_kernel(x_ref, w_ref, b_ref, o_ref):
    x = x_ref[...]
    xs = (x * _sigmoid(x)).astype(BF16)
    o_ref[0] = jnp.dot(xs, w_ref[0].astype(BF16), preferred_element_type=F32) + b_ref[0]


def _ada_call(cc, ada_w, ada_b):
    depth, d, nd = ada_w.shape
    tn = 1024
    return pl.pallas_call(
        _ada_kernel,
        grid=(depth, nd // tn),
        in_specs=[pl.BlockSpec((8, d), lambda l, j: (0, 0)),
                  pl.BlockSpec((1, d, tn), lambda l, j: (l, 0, j)),
                  pl.BlockSpec((1, 1, tn), lambda l, j: (l, 0, j))],
        out_specs=pl.BlockSpec((1, 8, tn), lambda l, j: (l, 0, j)),
        out_shape=jax.ShapeDtypeStruct((depth, 8, nd), F32),
        compiler_params=_cparams(2),
        name="ada_proj",
    )(cc, ada_w, ada_b.reshape(depth, 1, nd))


def _cast_kernel(x_ref, o_ref):
    o_ref[...] = x_ref[...].astype(o_ref.dtype)


def _cast_call(w, tr, tc):
    depth, r, c = w.shape
    spec = pl.BlockSpec((1, tr, tc), lambda l, i, j: (l, i, j))
    return pl.pallas_call(
        _cast_kernel,
        grid=(depth, r // tr, c // tc),
        in_specs=[spec],
        out_specs=spec,
        out_shape=jax.ShapeDtypeStruct(w.shape, BF16),
        compiler_params=_cparams(3),
        name="cast_w",
    )(w)


def _ffn_kernel(x_ref, g_ref, mod_ref, w1_ref, w3_ref, w2_ref, o_ref, xn_ref, rstd_ref, *, tm, nk):
    i = pl.program_id(0)
    k = pl.program_id(1)

    @pl.when(k == 0)
    def _():
        _norm_modulate_store(x_ref, g_ref, mod_ref, xn_ref, rstd_ref, i * tm, tm)
        o_ref[...] = jnp.zeros_like(o_ref)

    xn = xn_ref[...]
    a = jnp.dot(xn, w1_ref[...], preferred_element_type=F32)
    g = jnp.dot(xn, w3_ref[...], preferred_element_type=F32)
    act = (a * _sigmoid(a) * g).astype(BF16)
    o_ref[...] += jnp.dot(act, w2_ref[...], preferred_element_type=F32)

    @pl.when(k == nk - 1)
    def _():
        gate = jnp.where(_is_ctx_rows(i * tm, tm), mod_ref[5:6, :], mod_ref[2:3, :])
        o_ref[...] = x_ref[...] + 0.5 * gate * o_ref[...]


def _ffn_call(h, gains, mods, w13, w2, l, s):
    n, d = h.shape
    nk = D_FF // TF
    tm = TM_FFN
    return pl.pallas_call(
        functools.partial(_ffn_kernel, tm=tm, nk=nk),
        grid=(n // tm, nk),
        in_specs=[pl.BlockSpec((tm, d), lambda i, k: (i, 0)),
                  _gain_spec(l, s),
                  _mod_spec(l, s, 2),
                  pl.BlockSpec((None, d, TF), lambda i, k: (l, 0, k)),
                  pl.BlockSpec((None, d, TF), lambda i, k: (l, 0, k + nk)),
                  pl.BlockSpec((None, TF, d), lambda i, k: (l, k, 0))],
        out_specs=pl.BlockSpec((tm, d), lambda i, k: (i, 0)),
        out_shape=jax.ShapeDtypeStruct((n, d), F32),
        scratch_shapes=[pltpu.VMEM((tm, d), BF16), pltpu.VMEM((tm, LANES), F32)],
        compiler_params=_cparams(2),
        name="ffn",
    )(h, gains, mods, w13, w13, w2)


def _inproj_kernel(x_ref, g_ref, mod_ref, w_ref, wg_ref, pm_ref, pn_ref, gt_ref, xn_ref, rstd_ref, *, tm, n_main):
    i = pl.program_id(0)
    j = pl.program_id(1)

    @pl.when(j == 0)
    def _():
        _norm_modulate_store(x_ref, g_ref, mod_ref, xn_ref, rstd_ref, i * tm, tm)
        gt_ref[...] = lax.dot_general(wg_ref[...], xn_ref[...], NT_DIMS, preferred_element_type=F32)

    @pl.when(j < n_main)
    def _():
        pm_ref[...] = jnp.dot(xn_ref[...], w_ref[...], preferred_element_type=F32)

    @pl.when(j >= n_main)
    def _():
        pn_ref[...] = jnp.dot(xn_ref[...], w_ref[...], preferred_element_type=F32).astype(pn_ref.dtype)


def _inproj_call(h, gains, mods, w_all, w_gate, l):
    n, d = h.shape
    tm = TM_IN
    tn = TN_IN
    n_main = PM_COLS // tn
    n_na = PN_COLS // tn
    return pl.pallas_call(
        functools.partial(_inproj_kernel, tm=tm, n_main=n_main),
        grid=(n // tm, n_main + n_na),
        in_specs=[pl.BlockSpec((tm, d), lambda i, j: (i, 0)),
                  _gain_spec(l, 1),
                  _mod_spec(l, 1, 2),
                  pl.BlockSpec((None, d, tn), lambda i, j: (l, 0, j)),
                  pl.BlockSpec((None, LANES, d), lambda i, j: (l, 0, 0))],
        out_specs=[pl.BlockSpec((tm, tn), lambda i, j: (i, jnp.minimum(j, n_main - 1))),
                   pl.BlockSpec((tm, tn), lambda i, j: (i, jnp.maximum(j - n_main, 0))),
                   pl.BlockSpec((LANES, tm), lambda i, j: (0, i))],
        out_shape=[jax.ShapeDtypeStruct((n, PM_COLS), F32),
                   jax.ShapeDtypeStruct((n, PN_COLS), BF16),
                   jax.ShapeDtypeStruct((LANES, n), F32)],
        scratch_shapes=[pltpu.VMEM((tm, d), BF16), pltpu.VMEM((tm, LANES), F32)],
        compiler_params=_cparams(2),
        name="mix_inproj",
    )(h, gains, mods, w_all, w_gate)


def _gelu_tanh(x):
    c = np.float32(np.sqrt(2.0 / np.pi))
    return x * (0.5 * (1.0 + jnp.tanh(c * (x + 0.044715 * (x * x * x)))))


def _sgu_kernel(u_ref, v_ref, g_ref, w_ref, b_ref, o_ref):
    u = _gelu_tanh(u_ref[...])
    v = _rms(_gelu_tanh(v_ref[...]), g_ref[...]).astype(BF16)
    for grp in range(SGU_GROUPS):
        sl = slice(grp * SGU_CH, (grp + 1) * SGU_CH)
        mixed = jnp.dot(w_ref[grp].astype(BF16), v[:, sl], preferred_element_type=F32) + b_ref[:, sl]
        o_ref[:, sl] = (u[:, sl] * mixed).astype(o_ref.dtype)


def _sgu_call(pm, norm_g, w_s, b_full, l):
    n = pm.shape[0]
    c = SGU_CHUNK
    return pl.pallas_call(
        _sgu_kernel,
        grid=(n // c,),
        in_specs=[pl.BlockSpec((c, SGU_WIDTH), lambda i: (i, 0)),
                  pl.BlockSpec((c, SGU_WIDTH), lambda i: (i, 1)),
                  pl.BlockSpec((None, 1, SGU_WIDTH), lambda i: (l, 0, 0)),
                  pl.BlockSpec((None, SGU_GROUPS, c, c), lambda i: (l, 0, 0, 0)),
                  pl.BlockSpec((None, c, SGU_WIDTH), lambda i: (l, 0, 0))],
        out_specs=pl.BlockSpec((c, SGU_WIDTH), lambda i: (i, 0)),
        out_shape=jax.ShapeDtypeStruct((n, SGU_WIDTH), BF16),
        compiler_params=_cparams(1),
        name="sgu",
    )(pm, pm, norm_g, w_s, b_full)


def _prep_one(x_ref, prev_ref, next_ref, w_ref, b_ref, cos, sin, ext_ref, o_ref, zero_prev, zero_next,
              out_scale, tm, transposed):
    ext_ref[0:HALO, :] = jnp.where(zero_prev, 0.0, prev_ref[...])
    ext_ref[HALO:HALO + tm, :] = x_ref[...]
    ext_ref[HALO + tm:, :] = jnp.where(zero_next, 0.0, next_ref[...])
    half = MLSTM_CONV_W // 2
    y = b_ref[...] + w_ref[0:1, :] * ext_ref[HALO - half:HALO - half + tm, :]
    for j in range(1, MLSTM_CONV_W):
        y = y + w_ref[j:j + 1, :] * ext_ref[HALO - half + j:HALO - half + j + tm, :]
    y = y * _sigmoid(y)
    lane = lax.broadcasted_iota(jnp.int32, (tm, MLSTM_HD), 1)
    first = (lane % (MLSTM_HD // 2)) < (MLSTM_HD // 4)
    for hd in range(MLSTM_HEADS):
        sl = slice(hd * MLSTM_HD, (hd + 1) * MLSTM_HD)
        yh = y[:, sl]
        partner = jnp.where(first, pltpu.roll(yh, MLSTM_HD - MLSTM_HD // 4, 1),
                            pltpu.roll(yh, MLSTM_HD // 4, 1))
        val = (yh * cos + partner * sin) * out_scale
        if transposed:
            o_ref[sl, :] = val.T.astype(o_ref.dtype)
        else:
            o_ref[:, sl] = val.astype(o_ref.dtype)


def _mlstm_prep_kernel(q_ref, qp_ref, qn_ref, k_ref, kp_ref, kn_ref, wq_ref, wk_ref, bq_ref, bk_ref,
                       cos_ref, sin_ref, qo_ref, kt_ref, ext_ref, *, tm, n_tiles):
    i = pl.program_id(0)
    row0 = i * tm
    zero_prev = jnp.logical_or(row0 == 0, row0 == CTX_LEN)
    zero_next = jnp.logical_or(row0 + tm == CTX_LEN, i == n_tiles - 1)
    cos = cos_ref[...]
    sin = sin_ref[...]
    _prep_one(q_ref, qp_ref, qn_ref, wq_ref, bq_ref, cos, sin, ext_ref, qo_ref, zero_prev, zero_next,
              np.float32(MLSTM_HD ** -0.5), tm, False)
    _prep_one(k_ref, kp_ref, kn_ref, wk_ref, bk_ref, cos, sin, ext_ref, kt_ref, zero_prev, zero_next,
              np.float32(1.0), tm, True)


def _mlstm_prep_call(pm, conv_w, conv_b, cos_t, sin_t, l):
    n = pm.shape[0]
    tm = TM_PREP
    n_tiles = n // tm
    w = MLSTM_WIDTH
    hb = tm // HALO
    last_hb = n // HALO - 1
    q_col, k_col = 1, 2
    main = lambda col: pl.BlockSpec((tm, w), lambda i: (i, col))
    prev = lambda col: pl.BlockSpec((HALO, w), lambda i: (jnp.maximum(i * hb - 1, 0), col))
    nxt = lambda col: pl.BlockSpec((HALO, w), lambda i: (jnp.minimum((i + 1) * hb, last_hb), col))
    wspec = lambda col: pl.BlockSpec((None, MLSTM_CONV_W, w), lambda i: (l, 0, col))
    bspec = lambda col: pl.BlockSpec((None, 1, w), lambda i: (l, 0, col))
    tab = pl.BlockSpec((tm, MLSTM_HD), lambda i: (i, 0))
    return pl.pallas_call(
        functools.partial(_mlstm_prep_kernel, tm=tm, n_tiles=n_tiles),
        grid=(n_tiles,),
        in_specs=[main(q_col), prev(q_col), nxt(q_col), main(k_col), prev(k_col), nxt(k_col),
                  wspec(0), wspec(1), bspec(0), bspec(1), tab, tab],
        out_specs=[pl.BlockSpec((tm, w), lambda i: (i, 0)), pl.BlockSpec((w, tm), lambda i: (0, i))],
        out_shape=[jax.ShapeDtypeStruct((n, w), BF16), jax.ShapeDtypeStruct((w, n), BF16)],
        scratch_shapes=[pltpu.VMEM((tm + 2 * HALO, w), F32)],
        compiler_params=_cparams(1),
        name="mlstm_prep",
    )(pm, pm, pm, pm, pm, pm, conv_w, conv_w, conv_b, conv_b, cos_t, sin_t)


def _log_sigmoid(x):
    return jnp.minimum(x, 0.0) - jnp.log(1.0 + jnp.exp(-jnp.abs(x)))


def _split3(x):
    hi = x.astype(BF16)
    r1 = x - hi.astype(F32)
    mid = r1.astype(BF16)
    lo = (r1 - mid.astype(F32)).astype(BF16)
    return hi, mid, lo


def _mlstm_chunk(dd, r0, q_ref, kt_ref, v_ref, gr, o_ref, c_ref, n_ref, m_ref):
    L = MLSTM_CHUNK
    H = MLSTM_HEADS
    hd = MLSTM_HD
    r_idx = lax.broadcasted_iota(jnp.int32, (L, L), 0)
    c_idx = lax.broadcasted_iota(jnp.int32, (L, L), 1)
    tri = (c_idx <= r_idx) if dd == 0 else (c_idx >= r_idx)
    tri_t = (c_idx >= r_idx) if dd == 0 else (c_idx <= r_idx)
    tri_b = jnp.where(tri, 1.0, 0.0).astype(BF16)
    tri_t_b = jnp.where(tri_t, 1.0, 0.0).astype(BF16)
    tri3 = jnp.concatenate([tri_b, tri_b, tri_b], axis=1)

    base = 2 * H * dd
    i_row = gr[base:base + H, :]
    lf_all = _log_sigmoid(gr[base:base + 2 * H, :])
    lf_row = lf_all[H:2 * H, :]
    parts = _split3(lf_all)
    parts_f = [p.astype(F32)[H:2 * H, :] for p in parts]
    b_row = sum(jnp.dot(p, tri_t_b, preferred_element_type=F32) for p in parts)[H:2 * H, :]
    b_last = jnp.sum(lf_row, axis=-1, keepdims=True)
    ib_row = i_row - b_row
    g_row = b_last - b_row + i_row
    g_max = jnp.max(g_row, axis=-1, keepdims=True)

    for h in range(H):
        sl = slice(h * hd, (h + 1) * hd)
        q = q_ref[r0:r0 + L, sl]
        kt = kt_ref[sl, r0:r0 + L]
        v = v_ref[r0:r0 + L, sl].astype(BF16)
        m_old = m_ref[dd, h:h + 1, :]
        c_old = c_ref[dd, h]
        n_old = n_ref[dd, h]
        rows3 = jnp.concatenate([jnp.broadcast_to(p[h:h + 1, :], (LANES, L)) for p in parts_f],
                                axis=1).astype(BF16)
        b_col = lax.dot_general(tri3, rows3, NT_DIMS, preferred_element_type=F32)
        d_mat = jnp.where(tri, b_col + ib_row[h:h + 1, :], -jnp.inf)
        inter = b_col + m_old
        m_t = jnp.maximum(jnp.max(d_mat, axis=-1, keepdims=True), inter)
        qk = jnp.dot(q, kt, preferred_element_type=F32)
        s = qk * jnp.exp(d_mat - m_t)
        a = jnp.exp(inter - m_t)
        state_b = jnp.concatenate([c_old.astype(BF16), n_old.astype(BF16)], axis=1)
        qcn = jnp.dot(q, state_b, preferred_element_type=F32)
        num = a * qcn[:, :hd] + jnp.dot(s.astype(BF16), v, preferred_element_type=F32)
        den = a * qcn[:, hd:] + jnp.sum(s, axis=-1, keepdims=True)
        o_ref[r0:r0 + L, sl] = num * (1.0 / jnp.maximum(jnp.abs(den), jnp.exp(-m_t)))
        bl = b_last[h:h + 1, :]
        m_new = jnp.maximum(bl + m_old, g_max[h:h + 1, :])
        w_row = jnp.exp(g_row[h:h + 1, :] - m_new)
        decay = jnp.exp(bl + m_old - m_new)
        kw = kt.astype(F32) * w_row
        c_ref[dd, h] = decay * c_old + jnp.dot(kw.astype(BF16), v, preferred_element_type=F32)
        n_ref[dd, h] = decay * n_old + jnp.sum(kw, axis=-1, keepdims=True)
        m_ref[dd, h:h + 1, :] = m_new


def _mlstm_scan_kernel(qf_ref, ktf_ref, vf_ref, gf_ref, qb_ref, ktb_ref, vb_ref, gb_ref, bias_ref,
                       of_ref, ob_ref, c_ref, n_ref, m_ref):
    @pl.when(pl.program_id(0) == 0)
    def _():
        c_ref[...] = jnp.zeros_like(c_ref)
        n_ref[...] = jnp.zeros_like(n_ref)
        m_ref[...] = jnp.zeros_like(m_ref)

    L = MLSTM_CHUNK
    bias = bias_ref[...]
    for c in range(SCAN_CHUNKS):
        rf = c * L
        rb = (SCAN_CHUNKS - 1 - c) * L
        _mlstm_chunk(0, rf, qf_ref, ktf_ref, vf_ref, gf_ref[:, rf:rf + L] + bias, of_ref, c_ref, n_ref, m_ref)
        _mlstm_chunk(1, rb, qb_ref, ktb_ref, vb_ref, gb_ref[:, rb:rb + L] + bias, ob_ref, c_ref, n_ref, m_ref)


def _mlstm_bwd_block(j, n_blocks):
    return jnp.where(j == 0, 0, n_blocks - j)


def _mlstm_scan_call(qp, kt, pm, gates_t, gate_bias, l):
    n = qp.shape[0]
    L = SCAN_CHUNKS * MLSTM_CHUNK
    assert L == CTX_LEN
    nc = n // L
    w = MLSTM_WIDTH
    v_col = 3
    fwd = lambda j: j
    bwd = lambda j: _mlstm_bwd_block(j, nc)

    def specs(cidx):
        return [pl.BlockSpec((L, w), lambda j: (cidx(j), 0)),
                pl.BlockSpec((w, L), lambda j: (0, cidx(j))),
                pl.BlockSpec((L, w), lambda j: (cidx(j), v_col)),
                pl.BlockSpec((LANES, L), lambda j: (0, cidx(j)))]

    state = lambda last: pltpu.VMEM((2, MLSTM_HEADS, MLSTM_HD, last), F32)
    return pl.pallas_call(
        _mlstm_scan_kernel,
        grid=(nc,),
        in_specs=specs(fwd) + specs(bwd) + [pl.BlockSpec((None, LANES, 1), lambda j: (l, 0, 0))],
        out_specs=[pl.BlockSpec((L, w), lambda j: (fwd(j), 0)),
                   pl.BlockSpec((L, w), lambda j: (bwd(j), 0))],
        out_shape=[jax.ShapeDtypeStruct((n, w), F32), jax.ShapeDtypeStruct((n, w), F32)],
        scratch_shapes=[state(MLSTM_HD), state(LANES), pltpu.VMEM((2, MLSTM_HEADS, LANES), F32)],
        compiler_params=_cparams(1),
        name="mlstm_scan",
    )(qp, kt, pm, gates_t, qp, kt, pm, gates_t, gate_bias)


def _na_row_geometry(tt):
    ctx_blocks = CTX_LEN // GRID_W
    r = jnp.maximum(tt - ctx_blocks, 0)
    rs = jnp.clip(r - NA_WIN_H // 2, 0, GRID_ROWS - NA_WIN_H)
    return rs, jnp.where(tt < ctx_blocks, NA_WIN_H, r - rs)


def _na_kernel(q_ref, k_ref, v_ref, *rest):
    bias_refs, o_ref = rest[:NA_ROWS], rest[NA_ROWS]
    t = pl.program_id(0)
    nk = NA_WIN_H * GRID_W
    kc = k_ref[0:CTX_LEN, :]
    vc = v_ref[0:CTX_LEN, :]
    lo = lax.broadcasted_iota(jnp.int32, (GRID_W, LANES), 1) < NA_HD
    for rr in range(NA_ROWS):
        rs, _ = _na_row_geometry(t * NA_ROWS + rr)
        start = pl.multiple_of(CTX_LEN + rs * GRID_W, GRID_W)
        ks = k_ref[pl.ds(start, nk), :]
        vs = v_ref[pl.ds(start, nk), :]
        q = q_ref[rr * GRID_W:(rr + 1) * GRID_W, :] * (NA_HD ** -0.5)
        for pr in range(NA_HEADS // 2):
            sl = slice(pr * LANES, (pr + 1) * LANES)
            qp = q[:, sl]
            zero = jnp.zeros_like(qp)
            qs = jnp.concatenate([jnp.where(lo, qp, zero), jnp.where(lo, zero, qp)], axis=0)
            s_loc = lax.dot_general(qs, ks[:, sl], NT_DIMS, preferred_element_type=F32) + bias_refs[rr][0, pr]
            s_ctx = lax.dot_general(qs, kc[:, sl], NT_DIMS, preferred_element_type=F32)
            m = jnp.maximum(jnp.max(s_loc, axis=-1, keepdims=True), jnp.max(s_ctx, axis=-1, keepdims=True))
            e_loc = jnp.exp(s_loc - m)
            e_ctx = jnp.exp(s_ctx - m)
            den = jnp.sum(e_loc, axis=-1, keepdims=True) + jnp.sum(e_ctx, axis=-1, keepdims=True)
            o = (jnp.dot(e_loc.astype(BF16), vs[:, sl], preferred_element_type=F32)
                 + jnp.dot(e_ctx.astype(BF16), vc[:, sl], preferred_element_type=F32)) * (1.0 / den)
            o_ref[rr * GRID_W:(rr + 1) * GRID_W, sl] = jnp.where(lo, o[0:GRID_W], o[GRID_W:]).astype(o_ref.dtype)


def _na_call(pn, bias_tab, l):
    n = pn.shape[0]
    nk = NA_WIN_H * GRID_W
    tq = NA_ROWS * GRID_W

    def bias_spec(rr):
        return pl.BlockSpec((None, 1, NA_HEADS // 2, 2 * GRID_W, nk),
                            lambda t: (l, _na_row_geometry(t * NA_ROWS + rr)[1], 0, 0, 0))

    return pl.pallas_call(
        _na_kernel,
        grid=(n // tq,),
        in_specs=[pl.BlockSpec((tq, NA_WIDTH), lambda t: (t, 0)),
                  pl.BlockSpec((n, NA_WIDTH), lambda t: (0, 1)),
                  pl.BlockSpec((n, NA_WIDTH), lambda t: (0, 2))] + [bias_spec(rr) for rr in range(NA_ROWS)],
        out_specs=pl.BlockSpec((tq, NA_WIDTH), lambda t: (t, 0)),
        out_shape=jax.ShapeDtypeStruct((n, NA_WIDTH), BF16),
        compiler_params=_cparams(1),
        name="nbr_attn",
    )(pn, pn, pn, *([bias_tab] * NA_ROWS))


def _na_bias_tables(rpb_all):
    depth = rpb_all.shape[0]
    cols = np.arange(GRID_W)
    col_start = np.clip(cols - NA_WIN_W // 2, 0, GRID_W - NA_WIN_W)
    in_win = (cols[None, :] >= col_start[:, None]) & (cols[None, :] < col_start[:, None] + NA_WIN_W)
    dc = cols[None, :] - cols[:, None] + (NA_WIN_W - 1)
    onehot = np.zeros((GRID_W, GRID_W, 2 * NA_WIN_W - 1), np.float32)
    cc, kk = np.nonzero(in_win)
    onehot[cc, kk, dc[cc, kk]] = 1.0
    mask_bias = np.where(in_win, 0.0, NEG_BIG).astype(np.float32)
    n_dr = 2 * NA_WIN_H - 1
    e = jnp.einsum('lhrd,ckd->lhcrk', rpb_all, jnp.asarray(onehot), precision=lax.Precision.HIGHEST)
    e = (e + mask_bias[None, None, :, None, :]).reshape(depth, NA_HEADS, GRID_W, n_dr * GRID_W)
    nk = NA_WIN_H * GRID_W
    variants = [e[..., (NA_WIN_H - 1 - off) * GRID_W:(NA_WIN_H - 1 - off) * GRID_W + nk] for off in range(NA_WIN_H)]
    variants.append(jnp.full_like(variants[0], NEG_BIG))
    tab = jnp.stack(variants, axis=1)
    return tab.reshape(depth, NA_WIN_H + 1, NA_HEADS // 2, 2 * GRID_W, nk)


def _outproj_kernel(h_ref, a_ref, hf_ref, hb_ref, og_ref, ng_ref, c_ref, w_ref, mod_ref, o_ref, *, tm):
    i = pl.program_id(0)
    is_ctx = _is_ctx_rows(i * tm, tm)
    parts = []
    for hd in range(MLSTM_HEADS):
        sl = slice(hd * MLSTM_HD, (hd + 1) * MLSTM_HD)
        hh = _rms(hf_ref[:, sl] + hb_ref[:, sl], ng_ref[:, sl])
        parts.append((_sigmoid(og_ref[:, sl]) * hh).astype(BF16))
    b = jnp.concatenate(parts, axis=-1)
    y = jnp.dot(a_ref[...], w_ref[0:SGU_WIDTH, :], preferred_element_type=F32)
    y = y + jnp.dot(b, w_ref[SGU_WIDTH:SGU_WIDTH + MLSTM_WIDTH, :], preferred_element_type=F32)
    y = y + jnp.dot(c_ref[...], w_ref[SGU_WIDTH + MLSTM_WIDTH:, :], preferred_element_type=F32)
    gate = jnp.where(is_ctx, mod_ref[5:6, :], mod_ref[2:3, :])
    o_ref[...] = h_ref[...] + gate * y


def _outproj_call(h, a, hf, hb, pm, norm_g, c, w_out, mods, l):
    n, d = h.shape
    tm = TM_OUT
    w = MLSTM_WIDTH
    o_col = 4
    return pl.pallas_call(
        functools.partial(_outproj_kernel, tm=tm),
        grid=(n // tm,),
        in_specs=[pl.BlockSpec((tm, d), lambda i: (i, 0)),
                  pl.BlockSpec((tm, SGU_WIDTH), lambda i: (i, 0)),
                  pl.BlockSpec((tm, w), lambda i: (i, 0)),
                  pl.BlockSpec((tm, w), lambda i: (i, 0)),
                  pl.BlockSpec((tm, w), lambda i: (i, o_col)),
                  pl.BlockSpec((None, 1, w), lambda i: (l, 0, 0)),
                  pl.BlockSpec((tm, NA_WIDTH), lambda i: (i, 0)),
                  pl.BlockSpec((None, d, d), lambda i: (l, 0, 0)),
                  _mod_spec(l, 1, 1)],
        out_specs=pl.BlockSpec((tm, d), lambda i: (i, 0)),
        out_shape=jax.ShapeDtypeStruct((n, d), F32),
        compiler_params=_cparams(1),
        name="mix_outproj",
    )(h, a, hf, hb, pm, norm_g, c, w_out, mods)


def _final_norm_kernel(x_ref, g_ref, o_ref):
    o_ref[...] = _rms(x_ref[...], g_ref[...])


def _final_norm_call(h, g):
    n, d = h.shape
    tm = CTX_LEN
    return pl.pallas_call(
        _final_norm_kernel,
        grid=(SEQ // tm,),
        in_specs=[pl.BlockSpec((tm, d), lambda i: (i + 1, 0)),
                  pl.BlockSpec((1, d), lambda i: (0, 0))],
        out_specs=pl.BlockSpec((tm, d), lambda i: (i, 0)),
        out_shape=jax.ShapeDtypeStruct((SEQ, d), F32),
        compiler_params=_cparams(1),
        name="final_norm",
    )(h, g)


def _rope_tables():
    nf = MLSTM_HD // 4
    freqs = ROPE_THETA ** (-jnp.arange(nf, dtype=F32) / nf)
    t = jnp.arange(SEQ)
    row = (t // GRID_W).astype(F32)
    col = (t % GRID_W).astype(F32)
    ang_r = row[:, None] * freqs
    ang_c = col[:, None] * freqs
    cos = jnp.concatenate([jnp.cos(ang_r), jnp.cos(ang_r), jnp.cos(ang_c), jnp.cos(ang_c)], axis=-1)
    sin = jnp.concatenate([-jnp.sin(ang_r), jnp.sin(ang_r), -jnp.sin(ang_c), jnp.sin(ang_c)], axis=-1)
    cos = jnp.concatenate([jnp.ones((CTX_LEN, MLSTM_HD), F32), cos], axis=0)
    sin = jnp.concatenate([jnp.zeros((CTX_LEN, MLSTM_HD), F32), sin], axis=0)
    return cos, sin


def _mod_table(mods):
    m = mods.reshape(DEPTH, 2, 3, 3, D_MODEL).transpose(0, 2, 1, 3, 4).reshape(DEPTH, 3, 6, D_MODEL)
    return jnp.pad(m, ((0, 0), (0, 0), (0, 2), (0, 0)))


def kernel(x, c, ctx, c_ctx, ada_w, ada_b, norm_g, ffn1_w13, ffn1_w2, ffn2_w13, ffn2_w2, mix_w_in, mix_w_out,
           sgu_norm_g, sgu_w, sgu_b, mlstm_conv_w, mlstm_conv_b, mlstm_gate_b, mlstm_norm_g, na_rpb,
           final_norm_g):
    assert x.shape == (1, SEQ, D_MODEL) and ctx.shape == (1, CTX_LEN, D_MODEL)
    h = jnp.concatenate([ctx[0], x[0]], axis=0)
    cc = jnp.concatenate([c, c_ctx[None, :], jnp.zeros((6, D_MODEL), F32)], axis=0)
    mods = _mod_table(_ada_call(cc, ada_w, ada_b)[:, :2].reshape(DEPTH, 2, N_MOD, D_MODEL))
    gains = norm_g[:, :, None, :]
    cos_t, sin_t = _rope_tables()

    w13_1 = _cast_call(ffn1_w13, D_MODEL, 1024)
    w13_2 = _cast_call(ffn2_w13, D_MODEL, 1024)
    w2_1 = _cast_call(ffn1_w2, D_FF // 4, D_MODEL)
    w2_2 = _cast_call(ffn2_w2, D_FF // 4, D_MODEL)
    w_out = _cast_call(mix_w_out, D_MODEL // 2, D_MODEL)
    w_all = jnp.concatenate([mix_w_in[:, :, :PM_COLS], mix_w_in[:, :, PM_COLS + GATE_COLS:]], axis=2).astype(BF16)
    w_gate = jnp.pad(mix_w_in[:, :, PM_COLS:PM_COLS + GATE_COLS].transpose(0, 2, 1),
                     ((0, 0), (0, LANES - GATE_COLS), (0, 0))).astype(BF16)
    gate_bias = jnp.pad(mlstm_gate_b.reshape(DEPTH, GATE_COLS), ((0, 0), (0, LANES - GATE_COLS)))[:, :, None]
    sgu_bias = jnp.repeat(sgu_b.transpose(0, 2, 1), SGU_CH, axis=2)
    bias_tab = _na_bias_tables(na_rpb)
    conv_b = mlstm_conv_b[:, None, :]

    for l in range(DEPTH):
        h = _ffn_call(h, gains, mods, w13_1, w2_1, l, 0)
        pm, pn, gates_t = _inproj_call(h, gains, mods, w_all, w_gate, l)
        a = _sgu_call(pm, sgu_norm_g[:, None, :], sgu_w, sgu_bias, l)
        qp, kt = _mlstm_prep_call(pm, mlstm_conv_w, conv_b, cos_t, sin_t, l)
        hf, hb = _mlstm_scan_call(qp, kt, pm, gates_t, gate_bias, l)
        cattn = _na_call(pn, bias_tab, l)
        h = _outproj_call(h, a, hf, hb, pm, mlstm_norm_g[:, None, :], cattn, w_out, mods, l)
        h = _ffn_call(h, gains, mods, w13_2, w2_2, l, 2)

    return _final_norm_call(h, final_norm_g[None, :])[None]
```

```python
import functools

import numpy as np
import jax
import jax.numpy as jnp
from jax import lax
from jax.experimental import pallas as pl
from jax.experimental.pallas import tpu as pltpu

F32 = jnp.float32
BF16 = jnp.bfloat16

D_MODEL = 2048
SEQ = 8192
DEPTH = 4
GRID_W = 64
CTX_LEN = 256
N_TOK = CTX_LEN + SEQ
D_FF = 5632
N_MOD = 9
EPS = 1e-6
ROPE_THETA = 10000.0
SGU_WIDTH = 512
SGU_GROUPS = 4
SGU_CH = SGU_WIDTH // SGU_GROUPS
SGU_CHUNK = 128
NA_WIDTH = 512
NA_HEADS = 8
NA_HD = NA_WIDTH // NA_HEADS
NA_WIN_H = 8
NA_WIN_W = 16
MLSTM_WIDTH = 1024
MLSTM_HEADS = 8
MLSTM_HD = MLSTM_WIDTH // MLSTM_HEADS
MLSTM_CHUNK = 128
MLSTM_CONV_W = 5
GRID_ROWS = SEQ // GRID_W
LANES = 128
PM_COLS = 2 * SGU_WIDTH + 4 * MLSTM_WIDTH
PN_COLS = 3 * NA_WIDTH
GATE_COLS = 4 * MLSTM_HEADS
NEG_BIG = -1e30

VMEM_LIMIT = 60 * 1024 * 1024
TM_FFN = 528
TM_IN = 768
TM_OUT = 384
TF = 512
TN_MAIN = 1024
TN_IN = 512
TM_PREP = 256
HALO = 8
NORM_ROWS = 16
SCAN_CHUNKS = 2
NA_ROWS = 4
NA_SLAB = 12
NA_VAR_TOP, NA_VAR_MID, NA_VAR_BOTTOM, NA_VAR_CTX = range(4)

NT_DIMS = (((1,), (1,)), ((), ()))


def _cparams(n_axes):
    return pltpu.CompilerParams(dimension_semantics=("arbitrary",) * n_axes,
                                vmem_limit_bytes=VMEM_LIMIT)


def _sigmoid(x):
    return 1.0 / (1.0 + jnp.exp(-x))


def _rms(x, g):
    return x * lax.rsqrt(jnp.mean(x * x, axis=-1, keepdims=True) + EPS) * g


def _is_ctx_rows(row0, tm):
    rows = row0 + lax.broadcasted_iota(jnp.int32, (tm, 1), 0)
    return rows < CTX_LEN


def _norm_modulate_store(x_ref, g_ref, mod_ref, xn_ref, rstd_ref, row0, tm):
    def stats(r, carry):
        rs = pl.multiple_of(r * 8, 8)
        x = x_ref[pl.ds(rs, 8), :]
        rstd = lax.rsqrt(jnp.mean(x * x, axis=-1, keepdims=True) + EPS)
        rstd_ref[pl.ds(rs, 8), :] = jnp.broadcast_to(rstd, (8, LANES))
        return carry

    lax.fori_loop(0, tm // 8, stats, 0, unroll=tm // 48)

    g = g_ref[...]
    lat_scale = g * (1.0 + mod_ref[1:2, :])
    ctx_scale = g * (1.0 + mod_ref[4:5, :])
    lat_shift = mod_ref[0:1, :]
    ctx_shift = mod_ref[3:4, :]

    def strip(r, carry):
        rs = pl.multiple_of(r * NORM_ROWS, NORM_ROWS)
        x = x_ref[pl.ds(rs, NORM_ROWS), :]
        rstd = rstd_ref[pl.ds(rs, NORM_ROWS), 0:1]
        is_ctx = row0 + rs < CTX_LEN
        scale = jnp.where(is_ctx, ctx_scale, lat_scale)
        shift = jnp.where(is_ctx, ctx_shift, lat_shift)
        xn_ref[pl.ds(rs, NORM_ROWS), :] = ((x * rstd) * scale + shift).astype(xn_ref.dtype)
        return carry

    lax.fori_loop(0, tm // NORM_ROWS, strip, 0, unroll=3)


def _mod_spec(l, s, grid_rank):
    zeros = (0,) * 2
    return pl.BlockSpec((None, None, 8, D_MODEL), lambda *_: (l, s) + zeros)


def _gain_spec(l, s):
    return pl.BlockSpec((None, None, 1, D_MODEL), lambda *_: (l, s, 0, 0))


def _ada_kernel(x_ref, w_ref, b_ref, o_ref):
    x = x_ref[...]
    xs = (x * _sigmoid(x)).astype(BF16)
    o_ref[0] = jnp.dot(xs, w_ref[0].astype(BF16), preferred_element_type=F32) + b_ref[0]


def _ada_call(cc, ada_w, ada_b):
    depth, d, nd = ada_w.shape
    tn = 1024
    return pl.pallas_call(
        _ada_kernel,
        grid=(depth, nd // tn),
        in_specs=[pl.BlockSpec((8, d), lambda l, j: (0, 0)),
                  pl.BlockSpec((1, d, tn), lambda l, j: (l, 0, j)),
                  pl.BlockSpec((1, 1, tn), lambda l, j: (l, 0, j))],
        out_specs=pl.BlockSpec((1, 8, tn), lambda l, j: (l, 0, j)),
        out_shape=jax.ShapeDtypeStruct((depth, 8, nd), F32),
        compiler_params=_cparams(2),
        name="ada_proj",
    )(cc, ada_w, ada_b.reshape(depth, 1, nd))


def _cast_kernel(x_ref, o_ref):
    o_ref[...] = x_ref[...].astype(o_ref.dtype)


def _cast_call(w, tr, tc, rows=None):
    depth, r, c = w.shape
    rows = r if rows is None else rows
    spec = pl.BlockSpec((1, tr, tc), lambda l, i, j: (l, i, j))
    return pl.pallas_call(
        _cast_kernel,
        grid=(depth, rows // tr, c // tc),
        in_specs=[spec],
        out_specs=spec,
        out_shape=jax.ShapeDtypeStruct((depth, rows, c), BF16),
        compiler_params=_cparams(3),
        name="cast_w",
    )(w)


def _ffn_kernel(x_ref, g_ref, mod_ref, w1_ref, w3_ref, w2_ref, o_ref, xn_ref, rstd_ref, *, tm, nk):
    i = pl.program_id(0)
    k = pl.program_id(1)

    @pl.when(k == 0)
    def _():
        _norm_modulate_store(x_ref, g_ref, mod_ref, xn_ref, rstd_ref, i * tm, tm)
        o_ref[...] = jnp.zeros_like(o_ref)

    xn = xn_ref[...]
    a = jnp.dot(xn, w1_ref[...], preferred_element_type=F32)
    g = jnp.dot(xn, w3_ref[...], preferred_element_type=F32)
    act = (a * _sigmoid(a) * g).astype(BF16)
    o_ref[...] += jnp.dot(act, w2_ref[...], preferred_element_type=F32)

    @pl.when(k == nk - 1)
    def _():
        gate = jnp.where(_is_ctx_rows(i * tm, tm), mod_ref[5:6, :], mod_ref[2:3, :])
        o_ref[...] = x_ref[...] + 0.5 * gate * o_ref[...]


def _ffn_call(h, gains, mods, w13, w2, l, s):
    n, d = h.shape
    nk = D_FF // TF
    tm = TM_FFN
    return pl.pallas_call(
        functools.partial(_ffn_kernel, tm=tm, nk=nk),
        grid=(n // tm, nk),
        in_specs=[pl.BlockSpec((tm, d), lambda i, k: (i, 0)),
                  _gain_spec(l, s),
                  _mod_spec(l, s, 2),
                  pl.BlockSpec((None, d, TF), lambda i, k: (l, 0, k)),
                  pl.BlockSpec((None, d, TF), lambda i, k: (l, 0, k + nk)),
                  pl.BlockSpec((None, TF, d), lambda i, k: (l, k, 0))],
        out_specs=pl.BlockSpec((tm, d), lambda i, k: (i, 0)),
        out_shape=jax.ShapeDtypeStruct((n, d), F32),
        scratch_shapes=[pltpu.VMEM((tm, d), BF16), pltpu.VMEM((tm, LANES), F32)],
        compiler_params=_cparams(2),
        name="ffn",
    )(h, gains, mods, w13, w13, w2)


def _inproj_kernel(x_ref, g_ref, mod_ref, wm_ref, wn_ref, wg_ref, pm_ref, pn_ref, gt_ref, xn_ref, rstd_ref,
                   *, tm, n_main):
    i = pl.program_id(0)
    j = pl.program_id(1)

    @pl.when(j == 0)
    def _():
        _norm_modulate_store(x_ref, g_ref, mod_ref, xn_ref, rstd_ref, i * tm, tm)
        gt_ref[...] = lax.dot_general(wg_ref[...], xn_ref[...], NT_DIMS, preferred_element_type=F32)

    @pl.when(j < n_main)
    def _():
        pm_ref[...] = lax.dot_general(xn_ref[...], wm_ref[...], NT_DIMS, preferred_element_type=F32)

    @pl.when(j >= n_main)
    def _():
        pn_ref[...] = lax.dot_general(xn_ref[...], wn_ref[...], NT_DIMS,
                                      preferred_element_type=F32).astype(pn_ref.dtype)


def _inproj_call(h, gains, mods, w_main, w_na, w_gate, l):
    n, d = h.shape
    tm = TM_IN
    n_main = PM_COLS // TN_MAIN
    n_na = PN_COLS // TN_IN
    return pl.pallas_call(
        functools.partial(_inproj_kernel, tm=tm, n_main=n_main),
        grid=(n // tm, n_main + n_na),
        in_specs=[pl.BlockSpec((tm, d), lambda i, j: (i, 0)),
                  _gain_spec(l, 1),
                  _mod_spec(l, 1, 2),
                  pl.BlockSpec((None, TN_MAIN, d), lambda i, j: (l, jnp.minimum(j, n_main - 1), 0)),
                  pl.BlockSpec((None, TN_IN, d), lambda i, j: (l, jnp.maximum(j - n_main, 0), 0)),
                  pl.BlockSpec((None, LANES, d), lambda i, j: (l, 0, 0))],
        out_specs=[pl.BlockSpec((tm, TN_MAIN), lambda i, j: (i, jnp.minimum(j, n_main - 1))),
                   pl.BlockSpec((tm, TN_IN), lambda i, j: (i, jnp.maximum(j - n_main, 0))),
                   pl.BlockSpec((LANES, tm), lambda i, j: (0, i))],
        out_shape=[jax.ShapeDtypeStruct((n, PM_COLS), F32),
                   jax.ShapeDtypeStruct((n, PN_COLS), BF16),
                   jax.ShapeDtypeStruct((LANES, n), F32)],
        scratch_shapes=[pltpu.VMEM((tm, d), BF16), pltpu.VMEM((tm, LANES), F32)],
        compiler_params=_cparams(2),
        name="mix_inproj",
    )(h, gains, mods, w_main, w_na, w_gate)


def _gelu_tanh(x):
    c = np.float32(np.sqrt(2.0 / np.pi))
    return x * (0.5 * (1.0 + jnp.tanh(c * (x + 0.044715 * (x * x * x)))))


def _sgu_kernel(u_ref, v_ref, g_ref, w_ref, b_ref, o_ref):
    u = _gelu_tanh(u_ref[...])
    v = _rms(_gelu_tanh(v_ref[...]), g_ref[...]).astype(BF16)
    for grp in range(SGU_GROUPS):
        sl = slice(grp * SGU_CH, (grp + 1) * SGU_CH)
        mixed = jnp.dot(w_ref[grp].astype(BF16), v[:, sl], preferred_element_type=F32) + b_ref[:, sl]
        o_ref[:, sl] = (u[:, sl] * mixed).astype(o_ref.dtype)


def _sgu_call(pm, norm_g, w_s, b_full, l):
    n = pm.shape[0]
    c = SGU_CHUNK
    return pl.pallas_call(
        _sgu_kernel,
        grid=(n // c,),
        in_specs=[pl.BlockSpec((c, SGU_WIDTH), lambda i: (i, 0)),
                  pl.BlockSpec((c, SGU_WIDTH), lambda i: (i, 1)),
                  pl.BlockSpec((None, 1, SGU_WIDTH), lambda i: (l, 0, 0)),
                  pl.BlockSpec((None, SGU_GROUPS, c, c), lambda i: (l, 0, 0, 0)),
                  pl.BlockSpec((None, c, SGU_WIDTH), lambda i: (l, 0, 0))],
        out_specs=pl.BlockSpec((c, SGU_WIDTH), lambda i: (i, 0)),
        out_shape=jax.ShapeDtypeStruct((n, SGU_WIDTH), BF16),
        compiler_params=_cparams(1),
        name="sgu",
    )(pm, pm, norm_g, w_s, b_full)


def _prep_one(x_ref, prev_ref, next_ref, w_ref, b_ref, cos, sin, ext_ref, o_ref, zero_prev, zero_next,
              out_scale, tm, transposed):
    ext_ref[0:HALO, :] = jnp.where(zero_prev, 0.0, prev_ref[...])
    ext_ref[HALO:HALO + tm, :] = x_ref[...]
    ext_ref[HALO + tm:, :] = jnp.where(zero_next, 0.0, next_ref[...])
    half = MLSTM_CONV_W // 2
    y = b_ref[...] + w_ref[0:1, :] * ext_ref[HALO - half:HALO - half + tm, :]
    for j in range(1, MLSTM_CONV_W):
        y = y + w_ref[j:j + 1, :] * ext_ref[HALO - half + j:HALO - half + j + tm, :]
    y = y * _sigmoid(y)
    lane = lax.broadcasted_iota(jnp.int32, (tm, MLSTM_HD), 1)
    first = (lane % (MLSTM_HD // 2)) < (MLSTM_HD // 4)
    for hd in range(MLSTM_HEADS):
        sl = slice(hd * MLSTM_HD, (hd + 1) * MLSTM_HD)
        yh = y[:, sl]
        partner = jnp.where(first, pltpu.roll(yh, MLSTM_HD - MLSTM_HD // 4, 1),
                            pltpu.roll(yh, MLSTM_HD // 4, 1))
        val = (yh * cos + partner * sin) * out_scale
        if transposed:
            o_ref[sl, :] = val.T.astype(o_ref.dtype)
        else:
            o_ref[:, sl] = val.astype(o_ref.dtype)


def _mlstm_prep_kernel(q_ref, qp_ref, qn_ref, k_ref, kp_ref, kn_ref, wq_ref, wk_ref, bq_ref, bk_ref,
                       cos_ref, sin_ref, qo_ref, kt_ref, ext_ref, *, tm, n_tiles):
    i = pl.program_id(0)
    row0 = i * tm
    zero_prev = jnp.logical_or(row0 == 0, row0 == CTX_LEN)
    zero_next = jnp.logical_or(row0 + tm == CTX_LEN, i == n_tiles - 1)
    cos = cos_ref[...]
    sin = sin_ref[...]
    _prep_one(q_ref, qp_ref, qn_ref, wq_ref, bq_ref, cos, sin, ext_ref, qo_ref, zero_prev, zero_next,
              np.float32(MLSTM_HD ** -0.5), tm, False)
    _prep_one(k_ref, kp_ref, kn_ref, wk_ref, bk_ref, cos, sin, ext_ref, kt_ref, zero_prev, zero_next,
              np.float32(1.0), tm, True)


def _mlstm_prep_call(pm, conv_w, conv_b, cos_t, sin_t, l):
    n = pm.shape[0]
    tm = TM_PREP
    n_tiles = n // tm
    w = MLSTM_WIDTH
    hb = tm // HALO
    last_hb = n // HALO - 1
    q_col, k_col = 1, 2
    main = lambda col: pl.BlockSpec((tm, w), lambda i: (i, col))
    prev = lambda col: pl.BlockSpec((HALO, w), lambda i: (jnp.maximum(i * hb - 1, 0), col))
    nxt = lambda col: pl.BlockSpec((HALO, w), lambda i: (jnp.minimum((i + 1) * hb, last_hb), col))
    wspec = lambda col: pl.BlockSpec((None, MLSTM_CONV_W, w), lambda i: (l, 0, col))
    bspec = lambda col: pl.BlockSpec((None, 1, w), lambda i: (l, 0, col))
    tab = pl.BlockSpec((tm, MLSTM_HD), lambda i: (i, 0))
    return pl.pallas_call(
        functools.partial(_mlstm_prep_kernel, tm=tm, n_tiles=n_tiles),
        grid=(n_tiles,),
        in_specs=[main(q_col), prev(q_col), nxt(q_col), main(k_col), prev(k_col), nxt(k_col),
                  wspec(0), wspec(1), bspec(0), bspec(1), tab, tab],
        out_specs=[pl.BlockSpec((tm, w), lambda i: (i, 0)), pl.BlockSpec((w, tm), lambda i: (0, i))],
        out_shape=[jax.ShapeDtypeStruct((n, w), BF16), jax.ShapeDtypeStruct((w, n), BF16)],
        scratch_shapes=[pltpu.VMEM((tm + 2 * HALO, w), F32)],
        compiler_params=_cparams(1),
        name="mlstm_prep",
    )(pm, pm, pm, pm, pm, pm, conv_w, conv_w, conv_b, conv_b, cos_t, sin_t)


def _log_sigmoid(x):
    return jnp.minimum(x, 0.0) - jnp.log(1.0 + jnp.exp(-jnp.abs(x)))


def _split3(x):
    hi = x.astype(BF16)
    r1 = x - hi.astype(F32)
    mid = r1.astype(BF16)
    lo = (r1 - mid.astype(F32)).astype(BF16)
    return hi, mid, lo


def _mlstm_chunk(dd, r0, q_ref, kt_ref, v_ref, gr, o_ref, state):
    L = MLSTM_CHUNK
    H = MLSTM_HEADS
    hd = MLSTM_HD
    r_idx = lax.broadcasted_iota(jnp.int32, (L, L), 0)
    c_idx = lax.broadcasted_iota(jnp.int32, (L, L), 1)
    tri = (c_idx <= r_idx) if dd == 0 else (c_idx >= r_idx)
    tri_t = (c_idx >= r_idx) if dd == 0 else (c_idx <= r_idx)
    tri_b = jnp.where(tri, 1.0, 0.0).astype(BF16)
    tri_t_b = jnp.where(tri_t, 1.0, 0.0).astype(BF16)
    tri3 = jnp.concatenate([tri_b, tri_b, tri_b], axis=1)

    base = 2 * H * dd
    i_row = gr[base:base + H, :]
    lf_all = _log_sigmoid(gr[base:base + 2 * H, :])
    lf_row = lf_all[H:2 * H, :]
    parts = _split3(lf_all)
    b_row = sum(jnp.dot(p, tri_t_b, preferred_element_type=F32) for p in parts)[H:2 * H, :]
    b_cols = lax.dot_general(tri3, jnp.concatenate(parts, axis=1), NT_DIMS, preferred_element_type=F32)
    b_last = jnp.sum(lf_row, axis=-1, keepdims=True)
    ib_row = i_row - b_row
    g_row = b_last - b_row + i_row
    g_max = jnp.max(g_row, axis=-1, keepdims=True)

    new_state = []
    for h in range(H):
        sl = slice(h * hd, (h + 1) * hd)
        q = q_ref[r0:r0 + L, sl]
        kt = kt_ref[sl, r0:r0 + L]
        v = v_ref[r0:r0 + L, sl].astype(BF16)
        c_old, n_old, m_old = state[h]
        b_col = jnp.broadcast_to(b_cols[:, H + h:H + h + 1], (L, LANES))
        d_mat = jnp.where(tri, b_col + ib_row[h:h + 1, :], -jnp.inf)
        inter = b_col + m_old
        m_t = jnp.maximum(jnp.max(d_mat, axis=-1, keepdims=True), inter)
        qk = jnp.dot(q, kt, preferred_element_type=F32)
        s = qk * jnp.exp(d_mat - m_t)
        a = jnp.exp(inter - m_t)
        state_b = jnp.concatenate([c_old.astype(BF16), n_old.astype(BF16)], axis=1)
        qcn = jnp.dot(q, state_b, preferred_element_type=F32)
        bl = b_last[h:h + 1, :]
        m_new = jnp.maximum(bl + m_old, g_max[h:h + 1, :])
        w_row = jnp.exp(g_row[h:h + 1, :] - m_new)
        decay = jnp.exp(bl + m_old - m_new)
        kw = kt.astype(F32) * w_row
        sv = jnp.dot(jnp.concatenate([s.astype(BF16), kw.astype(BF16)], axis=0), v, preferred_element_type=F32)
        num = a * qcn[:, :hd] + sv[:L]
        den = a * qcn[:, hd:] + jnp.sum(s, axis=-1, keepdims=True)
        o_ref[r0:r0 + L, sl] = num * (1.0 / jnp.maximum(jnp.abs(den), jnp.exp(-m_t)))
        new_state.append((decay * c_old + sv[L:],
                          decay * n_old + jnp.sum(kw, axis=-1, keepdims=True),
                          m_new))
    return new_state


def _mlstm_scan_kernel(qf_ref, ktf_ref, vf_ref, gf_ref, qb_ref, ktb_ref, vb_ref, gb_ref, bias_ref,
                       of_ref, ob_ref, c_ref, n_ref, m_ref):
    @pl.when(pl.program_id(0) == 0)
    def _():
        c_ref[...] = jnp.zeros_like(c_ref)
        n_ref[...] = jnp.zeros_like(n_ref)
        m_ref[...] = jnp.zeros_like(m_ref)

    L = MLSTM_CHUNK
    H = MLSTM_HEADS
    bias = bias_ref[...]
    state = [[(c_ref[dd, h], n_ref[dd, h], m_ref[dd, h:h + 1, :]) for h in range(H)] for dd in range(2)]
    for c in range(SCAN_CHUNKS):
        rf = c * L
        rb = (SCAN_CHUNKS - 1 - c) * L
        state[0] = _mlstm_chunk(0, rf, qf_ref, ktf_ref, vf_ref, gf_ref[:, rf:rf + L] + bias, of_ref, state[0])
        state[1] = _mlstm_chunk(1, rb, qb_ref, ktb_ref, vb_ref, gb_ref[:, rb:rb + L] + bias, ob_ref, state[1])
    for dd in range(2):
        for h in range(H):
            c_ref[dd, h], n_ref[dd, h], m_ref[dd, h:h + 1, :] = state[dd][h]


def _mlstm_bwd_block(j, n_blocks):
    return jnp.where(j == 0, 0, n_blocks - j)


def _mlstm_scan_call(qp, kt, pm, gates_t, gate_bias, l):
    n = qp.shape[0]
    L = SCAN_CHUNKS * MLSTM_CHUNK
    assert L == CTX_LEN
    nc = n // L
    w = MLSTM_WIDTH
    v_col = 3
    fwd = lambda j: j
    bwd = lambda j: _mlstm_bwd_block(j, nc)

    def specs(cidx):
        return [pl.BlockSpec((L, w), lambda j: (cidx(j), 0)),
                pl.BlockSpec((w, L), lambda j: (0, cidx(j))),
                pl.BlockSpec((L, w), lambda j: (cidx(j), v_col)),
                pl.BlockSpec((LANES, L), lambda j: (0, cidx(j)))]

    state = lambda last: pltpu.VMEM((2, MLSTM_HEADS, MLSTM_HD, last), F32)
    return pl.pallas_call(
        _mlstm_scan_kernel,
        grid=(nc,),
        in_specs=specs(fwd) + specs(bwd) + [pl.BlockSpec((None, LANES, 1), lambda j: (l, 0, 0))],
        out_specs=[pl.BlockSpec((L, w), lambda j: (fwd(j), 0)),
                   pl.BlockSpec((L, w), lambda j: (bwd(j), 0))],
        out_shape=[jax.ShapeDtypeStruct((n, w), F32), jax.ShapeDtypeStruct((n, w), F32)],
        scratch_shapes=[state(MLSTM_HD), state(LANES), pltpu.VMEM((2, MLSTM_HEADS, LANES), F32)],
        compiler_params=_cparams(1),
        name="mlstm_scan",
    )(qp, kt, pm, gates_t, qp, kt, pm, gates_t, gate_bias)


def _na_block_geometry(t):
    r0 = jnp.maximum(t - 1, 0) * NA_ROWS
    s0 = jnp.clip(r0 - NA_WIN_H // 2, 0, GRID_ROWS - NA_SLAB)
    variant = jnp.where(t == 0, NA_VAR_CTX,
                        jnp.where(r0 == 0, NA_VAR_TOP, jnp.where(r0 == GRID_ROWS - NA_ROWS, NA_VAR_BOTTOM, NA_VAR_MID)))
    return s0, variant


def _na_kernel(q_ref, k_ref, v_ref, bias_ref, o_ref):
    t = pl.program_id(0)
    s0, _ = _na_block_geometry(t)
    nk = NA_SLAB * GRID_W
    start = pl.multiple_of(CTX_LEN + s0 * GRID_W, GRID_W)
    ks = k_ref[pl.ds(start, nk), :]
    vs = v_ref[pl.ds(start, nk), :]
    kc = k_ref[0:CTX_LEN, :]
    vc = v_ref[0:CTX_LEN, :]
    q = q_ref[...] * (NA_HD ** -0.5)
    lo = lax.broadcasted_iota(jnp.int32, (GRID_W, LANES), 1) < NA_HD
    for pr in range(NA_HEADS // 2):
        sl = slice(pr * LANES, (pr + 1) * LANES)
        pieces = []
        for rr in range(NA_ROWS):
            qp = q[rr * GRID_W:(rr + 1) * GRID_W, sl]
            zero = jnp.zeros_like(qp)
            pieces += [jnp.where(lo, qp, zero), jnp.where(lo, zero, qp)]
        qs = jnp.concatenate(pieces, axis=0)
        s_loc = lax.dot_general(qs, ks[:, sl], NT_DIMS, preferred_element_type=F32) + bias_ref[pr]
        s_ctx = lax.dot_general(qs, kc[:, sl], NT_DIMS, preferred_element_type=F32)
        m = jnp.maximum(jnp.max(s_loc, axis=-1, keepdims=True), jnp.max(s_ctx, axis=-1, keepdims=True))
        e_loc = jnp.exp(s_loc - m)
        e_ctx = jnp.exp(s_ctx - m)
        den = jnp.sum(e_loc, axis=-1, keepdims=True) + jnp.sum(e_ctx, axis=-1, keepdims=True)
        o = (jnp.dot(e_loc.astype(BF16), vs[:, sl], preferred_element_type=F32)
             + jnp.dot(e_ctx.astype(BF16), vc[:, sl], preferred_element_type=F32)) * (1.0 / den)
        for rr in range(NA_ROWS):
            top = o[2 * rr * GRID_W:(2 * rr + 1) * GRID_W]
            bot = o[(2 * rr + 1) * GRID_W:(2 * rr + 2) * GRID_W]
            o_ref[rr * GRID_W:(rr + 1) * GRID_W, sl] = jnp.where(lo, top, bot).astype(o_ref.dtype)


def _na_call(pn, bias_tab, l):
    n = pn.shape[0]
    nk = NA_SLAB * GRID_W
    tq = NA_ROWS * GRID_W
    assert tq == CTX_LEN and GRID_ROWS % NA_ROWS == 0
    return pl.pallas_call(
        _na_kernel,
        grid=(n // tq,),
        in_specs=[pl.BlockSpec((tq, NA_WIDTH), lambda t: (t, 0)),
                  pl.BlockSpec((n, NA_WIDTH), lambda t: (0, 1)),
                  pl.BlockSpec((n, NA_WIDTH), lambda t: (0, 2)),
                  pl.BlockSpec((None, None, NA_HEADS // 2, 2 * tq, nk),
                               lambda t: (l, _na_block_geometry(t)[1], 0, 0, 0))],
        out_specs=pl.BlockSpec((tq, NA_WIDTH), lambda t: (t, 0)),
        out_shape=jax.ShapeDtypeStruct((n, NA_WIDTH), BF16),
        compiler_params=_cparams(1),
        name="nbr_attn",
    )(pn, pn, pn, bias_tab)


def _na_bias_tables(rpb_all):
    depth = rpb_all.shape[0]
    cols = np.arange(GRID_W)
    col_start = np.clip(cols - NA_WIN_W // 2, 0, GRID_W - NA_WIN_W)
    in_win = (cols[None, :] >= col_start[:, None]) & (cols[None, :] < col_start[:, None] + NA_WIN_W)
    dc = cols[None, :] - cols[:, None] + (NA_WIN_W - 1)
    onehot = np.zeros((GRID_W, GRID_W, 2 * NA_WIN_W - 1), np.float32)
    cc, kk = np.nonzero(in_win)
    onehot[cc, kk, dc[cc, kk]] = 1.0
    mask_bias = np.where(in_win, 0.0, NEG_BIG).astype(np.float32)
    e = jnp.einsum('lhrd,ckd->lhcrk', rpb_all, jnp.asarray(onehot), precision=lax.Precision.HIGHEST)
    e = e + mask_bias[None, None, :, None, :]
    masked = jnp.full((depth, NA_HEADS, GRID_W, GRID_W), NEG_BIG, F32)
    first_rows = {NA_VAR_TOP: 0, NA_VAR_MID: 2 * NA_ROWS, NA_VAR_BOTTOM: GRID_ROWS - NA_ROWS}
    variants = []
    for var in (NA_VAR_TOP, NA_VAR_MID, NA_VAR_BOTTOM):
        r0 = first_rows[var]
        s0 = int(np.clip(r0 - NA_WIN_H // 2, 0, GRID_ROWS - NA_SLAB))
        per_row = []
        for rr in range(NA_ROWS):
            r = r0 + rr
            rs = int(np.clip(r - NA_WIN_H // 2, 0, GRID_ROWS - NA_WIN_H))
            blocks = []
            for j in range(NA_SLAB):
                key_row = s0 + j
                if rs <= key_row < rs + NA_WIN_H:
                    blocks.append(e[:, :, :, key_row - r + NA_WIN_H - 1, :])
                else:
                    blocks.append(masked)
            per_row.append(jnp.concatenate(blocks, axis=-1))
        variants.append(jnp.stack(per_row, axis=2))
    variants.append(jnp.full_like(variants[0], NEG_BIG))
    tab = jnp.stack(variants, axis=1)
    nk = NA_SLAB * GRID_W
    tab = tab.reshape(depth, 4, NA_HEADS // 2, 2, NA_ROWS, GRID_W, nk).transpose(0, 1, 2, 4, 3, 5, 6)
    return tab.reshape(depth, 4, NA_HEADS // 2, NA_ROWS * 2 * GRID_W, nk)


def _outproj_kernel(h_ref, a_ref, hf_ref, hb_ref, og_ref, ng_ref, c_ref, w_ref, mod_ref, o_ref, *, tm):
    i = pl.program_id(0)
    is_ctx = _is_ctx_rows(i * tm, tm)
    parts = []
    for hd in range(MLSTM_HEADS):
        sl = slice(hd * MLSTM_HD, (hd + 1) * MLSTM_HD)
        hh = _rms(hf_ref[:, sl] + hb_ref[:, sl], ng_ref[:, sl])
        parts.append((_sigmoid(og_ref[:, sl]) * hh).astype(BF16))
    b = jnp.concatenate(parts, axis=-1)
    y = jnp.dot(a_ref[...], w_ref[0:SGU_WIDTH, :], preferred_element_type=F32)
    y = y + jnp.dot(b, w_ref[SGU_WIDTH:SGU_WIDTH + MLSTM_WIDTH, :], preferred_element_type=F32)
    y = y + jnp.dot(c_ref[...], w_ref[SGU_WIDTH + MLSTM_WIDTH:, :], preferred_element_type=F32)
    gate = jnp.where(is_ctx, mod_ref[5:6, :], mod_ref[2:3, :])
    o_ref[...] = h_ref[...] + gate * y


def _outproj_call(h, a, hf, hb, pm, norm_g, c, w_out, mods, l):
    n, d = h.shape
    tm = TM_OUT
    w = MLSTM_WIDTH
    o_col = 4
    return pl.pallas_call(
        functools.partial(_outproj_kernel, tm=tm),
        grid=(n // tm,),
        in_specs=[pl.BlockSpec((tm, d), lambda i: (i, 0)),
                  pl.BlockSpec((tm, SGU_WIDTH), lambda i: (i, 0)),
                  pl.BlockSpec((tm, w), lambda i: (i, 0)),
                  pl.BlockSpec((tm, w), lambda i: (i, 0)),
                  pl.BlockSpec((tm, w), lambda i: (i, o_col)),
                  pl.BlockSpec((None, 1, w), lambda i: (l, 0, 0)),
                  pl.BlockSpec((tm, NA_WIDTH), lambda i: (i, 0)),
                  pl.BlockSpec((None, d, d), lambda i: (l, 0, 0)),
                  _mod_spec(l, 1, 1)],
        out_specs=pl.BlockSpec((tm, d), lambda i: (i, 0)),
        out_shape=jax.ShapeDtypeStruct((n, d), F32),
        compiler_params=_cparams(1),
        name="mix_outproj",
    )(h, a, hf, hb, pm, norm_g, c, w_out, mods)


def _final_norm_kernel(x_ref, g_ref, o_ref):
    o_ref[...] = _rms(x_ref[...], g_ref[...])


def _final_norm_call(h, g):
    n, d = h.shape
    tm = CTX_LEN
    return pl.pallas_call(
        _final_norm_kernel,
        grid=(SEQ // tm,),
        in_specs=[pl.BlockSpec((tm, d), lambda i: (i + 1, 0)),
                  pl.BlockSpec((1, d), lambda i: (0, 0))],
        out_specs=pl.BlockSpec((tm, d), lambda i: (i, 0)),
        out_shape=jax.ShapeDtypeStruct((SEQ, d), F32),
        compiler_params=_cparams(1),
        name="final_norm",
    )(h, g)


def _rope_tables():
    nf = MLSTM_HD // 4
    freqs = ROPE_THETA ** (-jnp.arange(nf, dtype=F32) / nf)
    t = jnp.arange(SEQ)
    row = (t // GRID_W).astype(F32)
    col = (t % GRID_W).astype(F32)
    ang_r = row[:, None] * freqs
    ang_c = col[:, None] * freqs
    cos = jnp.concatenate([jnp.cos(ang_r), jnp.cos(ang_r), jnp.cos(ang_c), jnp.cos(ang_c)], axis=-1)
    sin = jnp.concatenate([-jnp.sin(ang_r), jnp.sin(ang_r), -jnp.sin(ang_c), jnp.sin(ang_c)], axis=-1)
    cos = jnp.concatenate([jnp.ones((CTX_LEN, MLSTM_HD), F32), cos], axis=0)
    sin = jnp.concatenate([jnp.zeros((CTX_LEN, MLSTM_HD), F32), sin], axis=0)
    return cos, sin


def _mod_table(mods):
    m = mods.reshape(DEPTH, 2, 3, 3, D_MODEL).transpose(0, 2, 1, 3, 4).reshape(DEPTH, 3, 6, D_MODEL)
    return jnp.pad(m, ((0, 0), (0, 0), (0, 2), (0, 0)))


def kernel(x, c, ctx, c_ctx, ada_w, ada_b, norm_g, ffn1_w13, ffn1_w2, ffn2_w13, ffn2_w2, mix_w_in, mix_w_out,
           sgu_norm_g, sgu_w, sgu_b, mlstm_conv_w, mlstm_conv_b, mlstm_gate_b, mlstm_norm_g, na_rpb,
           final_norm_g):
    assert x.shape == (1, SEQ, D_MODEL) and ctx.shape == (1, CTX_LEN, D_MODEL)
    h = jnp.concatenate([ctx[0], x[0]], axis=0)
    cc = jnp.concatenate([c, c_ctx[None, :], jnp.zeros((6, D_MODEL), F32)], axis=0)
    mods = _mod_table(_ada_call(cc, ada_w, ada_b)[:, :2].reshape(DEPTH, 2, N_MOD, D_MODEL))
    gains = norm_g[:, :, None, :]
    cos_t, sin_t = _rope_tables()

    w13_1 = _cast_call(ffn1_w13, D_MODEL, 1024)
    w13_2 = _cast_call(ffn2_w13, D_MODEL, 1024)
    w2_1 = _cast_call(ffn1_w2, D_FF // 4, D_MODEL)
    w2_2 = _cast_call(ffn2_w2, D_FF // 4, D_MODEL)
    w_out = _cast_call(mix_w_out, D_MODEL // 2, D_MODEL)
    w_in_t = mix_w_in.transpose(0, 2, 1)
    w_main = _cast_call(w_in_t, TN_MAIN, D_MODEL, rows=PM_COLS)
    w_na = _cast_call(w_in_t[:, PM_COLS + GATE_COLS:, :], TN_IN, D_MODEL)
    w_gate = jnp.pad(w_in_t[:, PM_COLS:PM_COLS + GATE_COLS, :],
                     ((0, 0), (0, LANES - GATE_COLS), (0, 0))).astype(BF16)
    gate_bias = jnp.pad(mlstm_gate_b.reshape(DEPTH, GATE_COLS), ((0, 0), (0, LANES - GATE_COLS)))[:, :, None]
    sgu_bias = jnp.repeat(sgu_b.transpose(0, 2, 1), SGU_CH, axis=2)
    bias_tab = _na_bias_tables(na_rpb)
    conv_b = mlstm_conv_b[:, None, :]

    for l in range(DEPTH):
        h = _ffn_call(h, gains, mods, w13_1, w2_1, l, 0)
        pm, pn, gates_t = _inproj_call(h, gains, mods, w_main, w_na, w_gate, l)
        a = _sgu_call(pm, sgu_norm_g[:, None, :], sgu_w, sgu_bias, l)
        qp, kt = _mlstm_prep_call(pm, mlstm_conv_w, conv_b, cos_t, sin_t, l)
        hf, hb = _mlstm_scan_call(qp, kt, pm, gates_t, gate_bias, l)
        cattn = _na_call(pn, bias_tab, l)
        h = _outproj_call(h, a, hf, hb, pm, mlstm_norm_g[:, None, :], cattn, w_out, mods, l)
        h = _ffn_call(h, gains, mods, w13_2, w2_2, l, 2)

    return _final_norm_call(h, final_norm_g[None, :])[None]
```

```python
import functools

import numpy as np
import jax
import jax.numpy as jnp
from jax import lax
from jax.experimental import pallas as pl
from jax.experimental.pallas import tpu as pltpu

F32 = jnp.float32
BF16 = jnp.bfloat16

D_MODEL = 2048
SEQ = 8192
DEPTH = 4
GRID_W = 64
CTX_LEN = 256
N_TOK = CTX_LEN + SEQ
D_FF = 5632
N_MOD = 9
EPS = 1e-6
ROPE_THETA = 10000.0
SGU_WIDTH = 512
SGU_GROUPS = 4
SGU_CH = SGU_WIDTH // SGU_GROUPS
SGU_CHUNK = 128
NA_WIDTH = 512
NA_HEADS = 8
NA_HD = NA_WIDTH // NA_HEADS
NA_WIN_H = 8
NA_WIN_W = 16
MLSTM_WIDTH = 1024
MLSTM_HEADS = 8
MLSTM_HD = MLSTM_WIDTH // MLSTM_HEADS
MLSTM_CHUNK = 128
MLSTM_CONV_W = 5
GRID_ROWS = SEQ // GRID_W
LANES = 128
PM_COLS = 2 * SGU_WIDTH + 4 * MLSTM_WIDTH
PN_COLS = 3 * NA_WIDTH
GATE_COLS = 4 * MLSTM_HEADS
NEG_BIG = -1e30

VMEM_LIMIT = 60 * 1024 * 1024
TM_FFN = 768
TM_IN = 1408
TM_OUT = 384
TF = 512
TN_MAIN = 512
TN_IN = 512
TM_PREP = 256
HALO = 8
NORM_ROWS = 16
SCAN_CHUNKS = 2
NA_ROWS = 4
NA_SLAB = 12
NA_VAR_TOP, NA_VAR_MID, NA_VAR_BOTTOM, NA_VAR_CTX = range(4)

NT_DIMS = (((1,), (1,)), ((), ()))


def _cparams(n_axes):
    return pltpu.CompilerParams(dimension_semantics=("arbitrary",) * n_axes,
                                vmem_limit_bytes=VMEM_LIMIT)


def _sigmoid(x):
    return 1.0 / (1.0 + jnp.exp(-x))


def _rms(x, g):
    return x * lax.rsqrt(jnp.mean(x * x, axis=-1, keepdims=True) + EPS) * g


def _is_ctx_rows(row0, tm):
    rows = row0 + lax.broadcasted_iota(jnp.int32, (tm, 1), 0)
    return rows < CTX_LEN


def _norm_modulate_store(x_ref, g_ref, mod_ref, xn_ref, rstd_ref, row0, tm):
    def stats(r, carry):
        rs = pl.multiple_of(r * 8, 8)
        x = x_ref[pl.ds(rs, 8), :]
        rstd = lax.rsqrt(jnp.mean(x * x, axis=-1, keepdims=True) + EPS)
        rstd_ref[pl.ds(rs, 8), :] = jnp.broadcast_to(rstd, (8, LANES))
        return carry

    lax.fori_loop(0, tm // 8, stats, 0, unroll=next(u for u in (24, 22, 16, 12, 11, 8) if (tm // 8) % u == 0))

    g = g_ref[...]
    lat_scale = g * (1.0 + mod_ref[1:2, :])
    ctx_scale = g * (1.0 + mod_ref[4:5, :])
    lat_shift = mod_ref[0:1, :]
    ctx_shift = mod_ref[3:4, :]

    def strip(r, carry):
        rs = pl.multiple_of(r * NORM_ROWS, NORM_ROWS)
        x = x_ref[pl.ds(rs, NORM_ROWS), :]
        rstd = rstd_ref[pl.ds(rs, NORM_ROWS), 0:1]
        is_ctx = row0 + rs < CTX_LEN
        scale = jnp.where(is_ctx, ctx_scale, lat_scale)
        shift = jnp.where(is_ctx, ctx_shift, lat_shift)
        xn_ref[pl.ds(rs, NORM_ROWS), :] = ((x * rstd) * scale + shift).astype(xn_ref.dtype)
        return carry

    lax.fori_loop(0, tm // NORM_ROWS, strip, 0, unroll=next(u for u in (4, 3, 2, 1) if (tm // NORM_ROWS) % u == 0))


def _mod_spec(l, s, grid_rank):
    zeros = (0,) * 2
    return pl.BlockSpec((None, None, 8, D_MODEL), lambda *_: (l, s) + zeros)


def _gain_spec(l, s):
    return pl.BlockSpec((None, None, 1, D_MODEL), lambda *_: (l, s, 0, 0))


def _ada_kernel(x_ref, w_ref, b_ref, o_ref):
    x = x_ref[...]
    xs = (x * _sigmoid(x)).astype(BF16)
    o_ref[0] = jnp.dot(xs, w_ref[0].astype(BF16), preferred_element_type=F32) + b_ref[0]


def _ada_call(cc, ada_w, ada_b):
    depth, d, nd = ada_w.shape
    tn = 1024
    return pl.pallas_call(
        _ada_kernel,
        grid=(depth, nd // tn),
        in_specs=[pl.BlockSpec((8, d), lambda l, j: (0, 0)),
                  pl.BlockSpec((1, d, tn), lambda l, j: (l, 0, j)),
                  pl.BlockSpec((1, 1, tn), lambda l, j: (l, 0, j))],
        out_specs=pl.BlockSpec((1, 8, tn), lambda l, j: (l, 0, j)),
        out_shape=jax.ShapeDtypeStruct((depth, 8, nd), F32),
        compiler_params=_cparams(2),
        name="ada_proj",
    )(cc, ada_w, ada_b.reshape(depth, 1, nd))


def _cast_kernel(x_ref, o_ref):
    o_ref[...] = x_ref[...].astype(o_ref.dtype)


def _cast_call(w, tr, tc, rows=None):
    depth, r, c = w.shape
    rows = r if rows is None else rows
    spec = pl.BlockSpec((1, tr, tc), lambda l, i, j: (l, i, j))
    return pl.pallas_call(
        _cast_kernel,
        grid=(depth, rows // tr, c // tc),
        in_specs=[spec],
        out_specs=spec,
        out_shape=jax.ShapeDtypeStruct((depth, rows, c), BF16),
        compiler_params=_cparams(3),
        name="cast_w",
    )(w)


def _ffn_kernel(x_ref, g_ref, mod_ref, w1_ref, w3_ref, w2_ref, o_ref, xn_ref, rstd_ref, *, tm, nk):
    i = pl.program_id(0)
    k = pl.program_id(1)

    @pl.when(k == 0)
    def _():
        _norm_modulate_store(x_ref, g_ref, mod_ref, xn_ref, rstd_ref, i * tm, tm)
        o_ref[...] = jnp.zeros_like(o_ref)

    xn = xn_ref[...]
    a = jnp.dot(xn, w1_ref[...], preferred_element_type=F32)
    g = jnp.dot(xn, w3_ref[...], preferred_element_type=F32)
    act = (a * _sigmoid(a) * g).astype(BF16)
    o_ref[...] += jnp.dot(act, w2_ref[...], preferred_element_type=F32)

    @pl.when(k == nk - 1)
    def _():
        gate = jnp.where(_is_ctx_rows(i * tm, tm), mod_ref[5:6, :], mod_ref[2:3, :])
        o_ref[...] = x_ref[...] + 0.5 * gate * o_ref[...]


def _ffn_call(h, gains, mods, w13, w2, l, s):
    n, d = h.shape
    nk = D_FF // TF
    tm = TM_FFN
    return pl.pallas_call(
        functools.partial(_ffn_kernel, tm=tm, nk=nk),
        grid=(n // tm, nk),
        in_specs=[pl.BlockSpec((tm, d), lambda i, k: (i, 0)),
                  _gain_spec(l, s),
                  _mod_spec(l, s, 2),
                  pl.BlockSpec((None, d, TF), lambda i, k: (l, 0, k)),
                  pl.BlockSpec((None, d, TF), lambda i, k: (l, 0, k + nk)),
                  pl.BlockSpec((None, TF, d), lambda i, k: (l, k, 0))],
        out_specs=pl.BlockSpec((tm, d), lambda i, k: (i, 0)),
        out_shape=jax.ShapeDtypeStruct((n, d), F32),
        scratch_shapes=[pltpu.VMEM((tm, d), BF16), pltpu.VMEM((tm, LANES), F32)],
        compiler_params=_cparams(2),
        name="ffn",
    )(h, gains, mods, w13, w13, w2)


def _inproj_kernel(x_ref, g_ref, mod_ref, wm_ref, wn_ref, wg_ref, pm_ref, pn_ref, gt_ref, xn_ref, rstd_ref,
                   *, tm, n_main):
    i = pl.program_id(0)
    j = pl.program_id(1)

    @pl.when(j == 0)
    def _():
        _norm_modulate_store(x_ref, g_ref, mod_ref, xn_ref, rstd_ref, i * tm, tm)
        gt_ref[...] = lax.dot_general(wg_ref[...], xn_ref[...], NT_DIMS, preferred_element_type=F32)

    @pl.when(j < n_main)
    def _():
        pm_ref[...] = lax.dot_general(xn_ref[...], wm_ref[...], NT_DIMS, preferred_element_type=F32)

    @pl.when(j >= n_main)
    def _():
        pn_ref[...] = lax.dot_general(xn_ref[...], wn_ref[...], NT_DIMS,
                                      preferred_element_type=F32).astype(pn_ref.dtype)


def _inproj_call(h, gains, mods, w_main, w_na, w_gate, l):
    n, d = h.shape
    tm = TM_IN
    n_main = PM_COLS // TN_MAIN
    n_na = PN_COLS // TN_IN
    return pl.pallas_call(
        functools.partial(_inproj_kernel, tm=tm, n_main=n_main),
        grid=(n // tm, n_main + n_na),
        in_specs=[pl.BlockSpec((tm, d), lambda i, j: (i, 0)),
                  _gain_spec(l, 1),
                  _mod_spec(l, 1, 2),
                  pl.BlockSpec((None, TN_MAIN, d), lambda i, j: (l, jnp.minimum(j, n_main - 1), 0)),
                  pl.BlockSpec((None, TN_IN, d), lambda i, j: (l, jnp.maximum(j - n_main, 0), 0)),
                  pl.BlockSpec((None, LANES, d), lambda i, j: (l, 0, 0))],
        out_specs=[pl.BlockSpec((tm, TN_MAIN), lambda i, j: (i, jnp.minimum(j, n_main - 1))),
                   pl.BlockSpec((tm, TN_IN), lambda i, j: (i, jnp.maximum(j - n_main, 0))),
                   pl.BlockSpec((LANES, tm), lambda i, j: (0, i))],
        out_shape=[jax.ShapeDtypeStruct((n, PM_COLS), F32),
                   jax.ShapeDtypeStruct((n, PN_COLS), BF16),
                   jax.ShapeDtypeStruct((LANES, n), F32)],
        scratch_shapes=[pltpu.VMEM((tm, d), BF16), pltpu.VMEM((tm, LANES), F32)],
        compiler_params=_cparams(2),
        name="mix_inproj",
    )(h, gains, mods, w_main, w_na, w_gate)


def _gelu_tanh(x):
    c = np.float32(np.sqrt(2.0 / np.pi))
    return x * (0.5 * (1.0 + jnp.tanh(c * (x + 0.044715 * (x * x * x)))))


def _sgu_kernel(u_ref, v_ref, g_ref, w_ref, b_ref, o_ref):
    u = _gelu_tanh(u_ref[...])
    v = _rms(_gelu_tanh(v_ref[...]), g_ref[...]).astype(BF16)
    for grp in range(SGU_GROUPS):
        sl = slice(grp * SGU_CH, (grp + 1) * SGU_CH)
        mixed = jnp.dot(w_ref[grp].astype(BF16), v[:, sl], preferred_element_type=F32) + b_ref[:, sl]
        o_ref[:, sl] = (u[:, sl] * mixed).astype(o_ref.dtype)


def _sgu_call(pm, norm_g, w_s, b_full, l):
    n = pm.shape[0]
    c = SGU_CHUNK
    return pl.pallas_call(
        _sgu_kernel,
        grid=(n // c,),
        in_specs=[pl.BlockSpec((c, SGU_WIDTH), lambda i: (i, 0)),
                  pl.BlockSpec((c, SGU_WIDTH), lambda i: (i, 1)),
                  pl.BlockSpec((None, 1, SGU_WIDTH), lambda i: (l, 0, 0)),
                  pl.BlockSpec((None, SGU_GROUPS, c, c), lambda i: (l, 0, 0, 0)),
                  pl.BlockSpec((None, c, SGU_WIDTH), lambda i: (l, 0, 0))],
        out_specs=pl.BlockSpec((c, SGU_WIDTH), lambda i: (i, 0)),
        out_shape=jax.ShapeDtypeStruct((n, SGU_WIDTH), BF16),
        compiler_params=_cparams(1),
        name="sgu",
    )(pm, pm, norm_g, w_s, b_full)


def _prep_one(x_ref, prev_ref, next_ref, w_ref, b_ref, cos, sin, ext_ref, o_ref, zero_prev, zero_next,
              out_scale, tm, transposed):
    ext_ref[0:HALO, :] = jnp.where(zero_prev, 0.0, prev_ref[...])
    ext_ref[HALO:HALO + tm, :] = x_ref[...]
    ext_ref[HALO + tm:, :] = jnp.where(zero_next, 0.0, next_ref[...])
    half = MLSTM_CONV_W // 2
    y = b_ref[...] + w_ref[0:1, :] * ext_ref[HALO - half:HALO - half + tm, :]
    for j in range(1, MLSTM_CONV_W):
        y = y + w_ref[j:j + 1, :] * ext_ref[HALO - half + j:HALO - half + j + tm, :]
    y = y * _sigmoid(y)
    lane = lax.broadcasted_iota(jnp.int32, (tm, MLSTM_HD), 1)
    first = (lane % (MLSTM_HD // 2)) < (MLSTM_HD // 4)
    for hd in range(MLSTM_HEADS):
        sl = slice(hd * MLSTM_HD, (hd + 1) * MLSTM_HD)
        yh = y[:, sl]
        partner = jnp.where(first, pltpu.roll(yh, MLSTM_HD - MLSTM_HD // 4, 1),
                            pltpu.roll(yh, MLSTM_HD // 4, 1))
        val = (yh * cos + partner * sin) * out_scale
        if transposed:
            o_ref[sl, :] = val.T.astype(o_ref.dtype)
        else:
            o_ref[:, sl] = val.astype(o_ref.dtype)


def _mlstm_prep_kernel(q_ref, qp_ref, qn_ref, k_ref, kp_ref, kn_ref, wq_ref, wk_ref, bq_ref, bk_ref,
                       cos_ref, sin_ref, qo_ref, kt_ref, ext_ref, *, tm, n_tiles):
    i = pl.program_id(0)
    row0 = i * tm
    zero_prev = jnp.logical_or(row0 == 0, row0 == CTX_LEN)
    zero_next = jnp.logical_or(row0 + tm == CTX_LEN, i == n_tiles - 1)
    cos = cos_ref[...]
    sin = sin_ref[...]
    _prep_one(q_ref, qp_ref, qn_ref, wq_ref, bq_ref, cos, sin, ext_ref, qo_ref, zero_prev, zero_next,
              np.float32(MLSTM_HD ** -0.5), tm, False)
    _prep_one(k_ref, kp_ref, kn_ref, wk_ref, bk_ref, cos, sin, ext_ref, kt_ref, zero_prev, zero_next,
              np.float32(1.0), tm, True)


def _mlstm_prep_call(pm, conv_w, conv_b, cos_t, sin_t, l):
    n = pm.shape[0]
    tm = TM_PREP
    n_tiles = n // tm
    w = MLSTM_WIDTH
    hb = tm // HALO
    last_hb = n // HALO - 1
    q_col, k_col = 1, 2
    main = lambda col: pl.BlockSpec((tm, w), lambda i: (i, col))
    prev = lambda col: pl.BlockSpec((HALO, w), lambda i: (jnp.maximum(i * hb - 1, 0), col))
    nxt = lambda col: pl.BlockSpec((HALO, w), lambda i: (jnp.minimum((i + 1) * hb, last_hb), col))
    wspec = lambda col: pl.BlockSpec((None, MLSTM_CONV_W, w), lambda i: (l, 0, col))
    bspec = lambda col: pl.BlockSpec((None, 1, w), lambda i: (l, 0, col))
    tab = pl.BlockSpec((tm, MLSTM_HD), lambda i: (i, 0))
    return pl.pallas_call(
        functools.partial(_mlstm_prep_kernel, tm=tm, n_tiles=n_tiles),
        grid=(n_tiles,),
        in_specs=[main(q_col), prev(q_col), nxt(q_col), main(k_col), prev(k_col), nxt(k_col),
                  wspec(0), wspec(1), bspec(0), bspec(1), tab, tab],
        out_specs=[pl.BlockSpec((tm, w), lambda i: (i, 0)), pl.BlockSpec((w, tm), lambda i: (0, i))],
        out_shape=[jax.ShapeDtypeStruct((n, w), BF16), jax.ShapeDtypeStruct((w, n), BF16)],
        scratch_shapes=[pltpu.VMEM((tm + 2 * HALO, w), F32)],
        compiler_params=_cparams(1),
        name="mlstm_prep",
    )(pm, pm, pm, pm, pm, pm, conv_w, conv_w, conv_b, conv_b, cos_t, sin_t)


def _log_sigmoid(x):
    return jnp.minimum(x, 0.0) - jnp.log(1.0 + jnp.exp(-jnp.abs(x)))


def _split3(x):
    hi = x.astype(BF16)
    r1 = x - hi.astype(F32)
    mid = r1.astype(BF16)
    lo = (r1 - mid.astype(F32)).astype(BF16)
    return hi, mid, lo


def _mlstm_chunk(dd, r0, q_ref, kt_ref, v_ref, gr, o_ref, state):
    L = MLSTM_CHUNK
    H = MLSTM_HEADS
    hd = MLSTM_HD
    r_idx = lax.broadcasted_iota(jnp.int32, (L, L), 0)
    c_idx = lax.broadcasted_iota(jnp.int32, (L, L), 1)
    tri = (c_idx <= r_idx) if dd == 0 else (c_idx >= r_idx)
    tri_t = (c_idx >= r_idx) if dd == 0 else (c_idx <= r_idx)
    tri_b = jnp.where(tri, 1.0, 0.0).astype(BF16)
    tri_t_b = jnp.where(tri_t, 1.0, 0.0).astype(BF16)
    tri3 = jnp.concatenate([tri_b, tri_b, tri_b], axis=1)

    base = 2 * H * dd
    i_row = gr[base:base + H, :]
    lf_all = _log_sigmoid(gr[base:base + 2 * H, :])
    lf_row = lf_all[H:2 * H, :]
    parts = _split3(lf_all)
    b_row = sum(jnp.dot(p, tri_t_b, preferred_element_type=F32) for p in parts)[H:2 * H, :]
    b_cols = lax.dot_general(tri3, jnp.concatenate(parts, axis=1), NT_DIMS, preferred_element_type=F32)
    b_last = jnp.sum(lf_row, axis=-1, keepdims=True)
    ib_row = i_row - b_row
    g_row = b_last - b_row + i_row
    g_max = jnp.max(g_row, axis=-1, keepdims=True)

    new_state = []
    for h in range(H):
        sl = slice(h * hd, (h + 1) * hd)
        q = q_ref[r0:r0 + L, sl]
        kt = kt_ref[sl, r0:r0 + L]
        v = v_ref[r0:r0 + L, sl].astype(BF16)
        c_old, n_old, m_old = state[h]
        b_col = jnp.broadcast_to(b_cols[:, H + h:H + h + 1], (L, LANES))
        d_mat = jnp.where(tri, b_col + ib_row[h:h + 1, :], -jnp.inf)
        inter = b_col + m_old
        m_t = jnp.maximum(jnp.max(d_mat, axis=-1, keepdims=True), inter)
        qk = jnp.dot(q, kt, preferred_element_type=F32)
        s = qk * jnp.exp(d_mat - m_t)
        a = jnp.exp(inter - m_t)
        state_b = jnp.concatenate([c_old.astype(BF16), n_old.astype(BF16)], axis=1)
        qcn = jnp.dot(q, state_b, preferred_element_type=F32)
        bl = b_last[h:h + 1, :]
        m_new = jnp.maximum(bl + m_old, g_max[h:h + 1, :])
        w_row = jnp.exp(g_row[h:h + 1, :] - m_new)
        decay = jnp.exp(bl + m_old - m_new)
        kw = kt.astype(F32) * w_row
        sv = jnp.dot(jnp.concatenate([s.astype(BF16), kw.astype(BF16)], axis=0), v, preferred_element_type=F32)
        num = a * qcn[:, :hd] + sv[:L]
        den = a * qcn[:, hd:] + jnp.sum(s, axis=-1, keepdims=True)
        o_ref[r0:r0 + L, sl] = num * (1.0 / jnp.maximum(jnp.abs(den), jnp.exp(-m_t)))
        new_state.append((decay * c_old + sv[L:],
                          decay * n_old + jnp.sum(kw, axis=-1, keepdims=True),
                          m_new))
    return new_state


def _mlstm_scan_kernel(qf_ref, ktf_ref, vf_ref, gf_ref, qb_ref, ktb_ref, vb_ref, gb_ref, bias_ref,
                       of_ref, ob_ref, c_ref, n_ref, m_ref):
    @pl.when(pl.program_id(0) == 0)
    def _():
        c_ref[...] = jnp.zeros_like(c_ref)
        n_ref[...] = jnp.zeros_like(n_ref)
        m_ref[...] = jnp.zeros_like(m_ref)

    L = MLSTM_CHUNK
    H = MLSTM_HEADS
    bias = bias_ref[...]
    state = [[(c_ref[dd, h], n_ref[dd, h], m_ref[dd, h:h + 1, :]) for h in range(H)] for dd in range(2)]
    for c in range(SCAN_CHUNKS):
        rf = c * L
        rb = (SCAN_CHUNKS - 1 - c) * L
        state[0] = _mlstm_chunk(0, rf, qf_ref, ktf_ref, vf_ref, gf_ref[:, rf:rf + L] + bias, of_ref, state[0])
        state[1] = _mlstm_chunk(1, rb, qb_ref, ktb_ref, vb_ref, gb_ref[:, rb:rb + L] + bias, ob_ref, state[1])
    for dd in range(2):
        for h in range(H):
            c_ref[dd, h], n_ref[dd, h], m_ref[dd, h:h + 1, :] = state[dd][h]


def _mlstm_bwd_block(j, n_blocks):
    return jnp.where(j == 0, 0, n_blocks - j)


def _mlstm_scan_call(qp, kt, pm, gates_t, gate_bias, l):
    n = qp.shape[0]
    L = SCAN_CHUNKS * MLSTM_CHUNK
    assert L == CTX_LEN
    nc = n // L
    w = MLSTM_WIDTH
    v_col = 3
    fwd = lambda j: j
    bwd = lambda j: _mlstm_bwd_block(j, nc)

    def specs(cidx):
        return [pl.BlockSpec((L, w), lambda j: (cidx(j), 0)),
                pl.BlockSpec((w, L), lambda j: (0, cidx(j))),
                pl.BlockSpec((L, w), lambda j: (cidx(j), v_col)),
                pl.BlockSpec((LANES, L), lambda j: (0, cidx(j)))]

    state = lambda last: pltpu.VMEM((2, MLSTM_HEADS, MLSTM_HD, last), F32)
    return pl.pallas_call(
        _mlstm_scan_kernel,
        grid=(nc,),
        in_specs=specs(fwd) + specs(bwd) + [pl.BlockSpec((None, LANES, 1), lambda j: (l, 0, 0))],
        out_specs=[pl.BlockSpec((L, w), lambda j: (fwd(j), 0)),
                   pl.BlockSpec((L, w), lambda j: (bwd(j), 0))],
        out_shape=[jax.ShapeDtypeStruct((n, w), F32), jax.ShapeDtypeStruct((n, w), F32)],
        scratch_shapes=[state(MLSTM_HD), state(LANES), pltpu.VMEM((2, MLSTM_HEADS, LANES), F32)],
        compiler_params=_cparams(1),
        name="mlstm_scan",
    )(qp, kt, pm, gates_t, qp, kt, pm, gates_t, gate_bias)


def _na_block_geometry(t):
    r0 = jnp.maximum(t - 1, 0) * NA_ROWS
    s0 = jnp.clip(r0 - NA_WIN_H // 2, 0, GRID_ROWS - NA_SLAB)
    variant = jnp.where(t == 0, NA_VAR_CTX,
                        jnp.where(r0 == 0, NA_VAR_TOP, jnp.where(r0 == GRID_ROWS - NA_ROWS, NA_VAR_BOTTOM, NA_VAR_MID)))
    return s0, variant


def _na_kernel(q_ref, k_ref, v_ref, bias_ref, o_ref):
    t = pl.program_id(0)
    s0, _ = _na_block_geometry(t)
    nk = NA_SLAB * GRID_W
    start = pl.multiple_of(CTX_LEN + s0 * GRID_W, GRID_W)
    ks = k_ref[pl.ds(start, nk), :]
    vs = v_ref[pl.ds(start, nk), :]
    kc = k_ref[0:CTX_LEN, :]
    vc = v_ref[0:CTX_LEN, :]
    q = q_ref[...] * (NA_HD ** -0.5)
    lo = lax.broadcasted_iota(jnp.int32, (GRID_W, LANES), 1) < NA_HD
    for pr in range(NA_HEADS // 2):
        sl = slice(pr * LANES, (pr + 1) * LANES)
        pieces = []
        for rr in range(NA_ROWS):
            qp = q[rr * GRID_W:(rr + 1) * GRID_W, sl]
            zero = jnp.zeros_like(qp)
            pieces += [jnp.where(lo, qp, zero), jnp.where(lo, zero, qp)]
        qs = jnp.concatenate(pieces, axis=0)
        s_loc = lax.dot_general(qs, ks[:, sl], NT_DIMS, preferred_element_type=F32) + bias_ref[pr]
        s_ctx = lax.dot_general(qs, kc[:, sl], NT_DIMS, preferred_element_type=F32)
        m = jnp.maximum(jnp.max(s_loc, axis=-1, keepdims=True), jnp.max(s_ctx, axis=-1, keepdims=True))
        e_loc = jnp.exp(s_loc - m)
        e_ctx = jnp.exp(s_ctx - m)
        den = jnp.sum(e_loc, axis=-1, keepdims=True) + jnp.sum(e_ctx, axis=-1, keepdims=True)
        o = (jnp.dot(e_loc.astype(BF16), vs[:, sl], preferred_element_type=F32)
             + jnp.dot(e_ctx.astype(BF16), vc[:, sl], preferred_element_type=F32)) * (1.0 / den)
        for rr in range(NA_ROWS):
            top = o[2 * rr * GRID_W:(2 * rr + 1) * GRID_W]
            bot = o[(2 * rr + 1) * GRID_W:(2 * rr + 2) * GRID_W]
            o_ref[rr * GRID_W:(rr + 1) * GRID_W, sl] = jnp.where(lo, top, bot).astype(o_ref.dtype)


def _na_call(pn, bias_tab, l):
    n = pn.shape[0]
    nk = NA_SLAB * GRID_W
    tq = NA_ROWS * GRID_W
    assert tq == CTX_LEN and GRID_ROWS % NA_ROWS == 0
    return pl.pallas_call(
        _na_kernel,
        grid=(n // tq,),
        in_specs=[pl.BlockSpec((tq, NA_WIDTH), lambda t: (t, 0)),
                  pl.BlockSpec((n, NA_WIDTH), lambda t: (0, 1)),
                  pl.BlockSpec((n, NA_WIDTH), lambda t: (0, 2)),
                  pl.BlockSpec((None, None, NA_HEADS // 2, 2 * tq, nk),
                               lambda t: (l, _na_block_geometry(t)[1], 0, 0, 0))],
        out_specs=pl.BlockSpec((tq, NA_WIDTH), lambda t: (t, 0)),
        out_shape=jax.ShapeDtypeStruct((n, NA_WIDTH), BF16),
        compiler_params=_cparams(1),
        name="nbr_attn",
    )(pn, pn, pn, bias_tab)


def _na_bias_tables(rpb_all):
    depth = rpb_all.shape[0]
    cols = np.arange(GRID_W)
    col_start = np.clip(cols - NA_WIN_W // 2, 0, GRID_W - NA_WIN_W)
    in_win = (cols[None, :] >= col_start[:, None]) & (cols[None, :] < col_start[:, None] + NA_WIN_W)
    dc = cols[None, :] - cols[:, None] + (NA_WIN_W - 1)
    onehot = np.zeros((GRID_W, GRID_W, 2 * NA_WIN_W - 1), np.float32)
    cc, kk = np.nonzero(in_win)
    onehot[cc, kk, dc[cc, kk]] = 1.0
    mask_bias = np.where(in_win, 0.0, NEG_BIG).astype(np.float32)
    e = jnp.einsum('lhrd,ckd->lhcrk', rpb_all, jnp.asarray(onehot), precision=lax.Precision.HIGHEST)
    e = e + mask_bias[None, None, :, None, :]
    masked = jnp.full((depth, NA_HEADS, GRID_W, GRID_W), NEG_BIG, F32)
    first_rows = {NA_VAR_TOP: 0, NA_VAR_MID: 2 * NA_ROWS, NA_VAR_BOTTOM: GRID_ROWS - NA_ROWS}
    variants = []
    for var in (NA_VAR_TOP, NA_VAR_MID, NA_VAR_BOTTOM):
        r0 = first_rows[var]
        s0 = int(np.clip(r0 - NA_WIN_H // 2, 0, GRID_ROWS - NA_SLAB))
        per_row = []
        for rr in range(NA_ROWS):
            r = r0 + rr
            rs = int(np.clip(r - NA_WIN_H // 2, 0, GRID_ROWS - NA_WIN_H))
            blocks = []
            for j in range(NA_SLAB):
                key_row = s0 + j
                if rs <= key_row < rs + NA_WIN_H:
                    blocks.append(e[:, :, :, key_row - r + NA_WIN_H - 1, :])
                else:
                    blocks.append(masked)
            per_row.append(jnp.concatenate(blocks, axis=-1))
        variants.append(jnp.stack(per_row, axis=2))
    variants.append(jnp.full_like(variants[0], NEG_BIG))
    tab = jnp.stack(variants, axis=1)
    nk = NA_SLAB * GRID_W
    tab = tab.reshape(depth, 4, NA_HEADS // 2, 2, NA_ROWS, GRID_W, nk).transpose(0, 1, 2, 4, 3, 5, 6)
    return tab.reshape(depth, 4, NA_HEADS // 2, NA_ROWS * 2 * GRID_W, nk)


def _outproj_kernel(h_ref, a_ref, hf_ref, hb_ref, og_ref, ng_ref, c_ref, w_ref, mod_ref, o_ref, *, tm):
    i = pl.program_id(0)
    is_ctx = _is_ctx_rows(i * tm, tm)
    parts = []
    for hd in range(MLSTM_HEADS):
        sl = slice(hd * MLSTM_HD, (hd + 1) * MLSTM_HD)
        hh = _rms(hf_ref[:, sl] + hb_ref[:, sl], ng_ref[:, sl])
        parts.append((_sigmoid(og_ref[:, sl]) * hh).astype(BF16))
    b = jnp.concatenate(parts, axis=-1)
    y = jnp.dot(a_ref[...], w_ref[0:SGU_WIDTH, :], preferred_element_type=F32)
    y = y + jnp.dot(b, w_ref[SGU_WIDTH:SGU_WIDTH + MLSTM_WIDTH, :], preferred_element_type=F32)
    y = y + jnp.dot(c_ref[...], w_ref[SGU_WIDTH + MLSTM_WIDTH:, :], preferred_element_type=F32)
    gate = jnp.where(is_ctx, mod_ref[5:6, :], mod_ref[2:3, :])
    o_ref[...] = h_ref[...] + gate * y


def _outproj_call(h, a, hf, hb, pm, norm_g, c, w_out, mods, l):
    n, d = h.shape
    tm = TM_OUT
    w = MLSTM_WIDTH
    o_col = 4
    return pl.pallas_call(
        functools.partial(_outproj_kernel, tm=tm),
        grid=(n // tm,),
        in_specs=[pl.BlockSpec((tm, d), lambda i: (i, 0)),
                  pl.BlockSpec((tm, SGU_WIDTH), lambda i: (i, 0)),
                  pl.BlockSpec((tm, w), lambda i: (i, 0)),
                  pl.BlockSpec((tm, w), lambda i: (i, 0)),
                  pl.BlockSpec((tm, w), lambda i: (i, o_col)),
                  pl.BlockSpec((None, 1, w), lambda i: (l, 0, 0)),
                  pl.BlockSpec((tm, NA_WIDTH), lambda i: (i, 0)),
                  pl.BlockSpec((None, d, d), lambda i: (l, 0, 0)),
                  _mod_spec(l, 1, 1)],
        out_specs=pl.BlockSpec((tm, d), lambda i: (i, 0)),
        out_shape=jax.ShapeDtypeStruct((n, d), F32),
        compiler_params=_cparams(1),
        name="mix_outproj",
    )(h, a, hf, hb, pm, norm_g, c, w_out, mods)


def _final_norm_kernel(x_ref, g_ref, o_ref):
    o_ref[...] = _rms(x_ref[...], g_ref[...])


def _final_norm_call(h, g):
    n, d = h.shape
    tm = CTX_LEN
    return pl.pallas_call(
        _final_norm_kernel,
        grid=(SEQ // tm,),
        in_specs=[pl.BlockSpec((tm, d), lambda i: (i + 1, 0)),
                  pl.BlockSpec((1, d), lambda i: (0, 0))],
        out_specs=pl.BlockSpec((tm, d), lambda i: (i, 0)),
        out_shape=jax.ShapeDtypeStruct((SEQ, d), F32),
        compiler_params=_cparams(1),
        name="final_norm",
    )(h, g)


def _rope_tables():
    nf = MLSTM_HD // 4
    freqs = ROPE_THETA ** (-jnp.arange(nf, dtype=F32) / nf)
    t = jnp.arange(SEQ)
    row = (t // GRID_W).astype(F32)
    col = (t % GRID_W).astype(F32)
    ang_r = row[:, None] * freqs
    ang_c = col[:, None] * freqs
    cos = jnp.concatenate([jnp.cos(ang_r), jnp.cos(ang_r), jnp.cos(ang_c), jnp.cos(ang_c)], axis=-1)
    sin = jnp.concatenate([-jnp.sin(ang_r), jnp.sin(ang_r), -jnp.sin(ang_c), jnp.sin(ang_c)], axis=-1)
    cos = jnp.concatenate([jnp.ones((CTX_LEN, MLSTM_HD), F32), cos], axis=0)
    sin = jnp.concatenate([jnp.zeros((CTX_LEN, MLSTM_HD), F32), sin], axis=0)
    return cos, sin


def _mod_table(mods):
    m = mods.reshape(DEPTH, 2, 3, 3, D_MODEL).transpose(0, 2, 1, 3, 4).reshape(DEPTH, 3, 6, D_MODEL)
    return jnp.pad(m, ((0, 0), (0, 0), (0, 2), (0, 0)))


def kernel(x, c, ctx, c_ctx, ada_w, ada_b, norm_g, ffn1_w13, ffn1_w2, ffn2_w13, ffn2_w2, mix_w_in, mix_w_out,
           sgu_norm_g, sgu_w, sgu_b, mlstm_conv_w, mlstm_conv_b, mlstm_gate_b, mlstm_norm_g, na_rpb,
           final_norm_g):
    assert x.shape == (1, SEQ, D_MODEL) and ctx.shape == (1, CTX_LEN, D_MODEL)
    h = jnp.concatenate([ctx[0], x[0]], axis=0)
    cc = jnp.concatenate([c, c_ctx[None, :], jnp.zeros((6, D_MODEL), F32)], axis=0)
    mods = _mod_table(_ada_call(cc, ada_w, ada_b)[:, :2].reshape(DEPTH, 2, N_MOD, D_MODEL))
    gains = norm_g[:, :, None, :]
    cos_t, sin_t = _rope_tables()

    w13_1 = _cast_call(ffn1_w13, D_MODEL, 1024)
    w13_2 = _cast_call(ffn2_w13, D_MODEL, 1024)
    w2_1 = _cast_call(ffn1_w2, D_FF // 4, D_MODEL)
    w2_2 = _cast_call(ffn2_w2, D_FF // 4, D_MODEL)
    w_out = _cast_call(mix_w_out, D_MODEL // 2, D_MODEL)
    w_in_t = mix_w_in.transpose(0, 2, 1)
    w_main = _cast_call(w_in_t, TN_MAIN, D_MODEL, rows=PM_COLS)
    w_na = _cast_call(w_in_t[:, PM_COLS + GATE_COLS:, :], TN_IN, D_MODEL)
    w_gate = jnp.pad(w_in_t[:, PM_COLS:PM_COLS + GATE_COLS, :],
                     ((0, 0), (0, LANES - GATE_COLS), (0, 0))).astype(BF16)
    gate_bias = jnp.pad(mlstm_gate_b.reshape(DEPTH, GATE_COLS), ((0, 0), (0, LANES - GATE_COLS)))[:, :, None]
    sgu_bias = jnp.repeat(sgu_b.transpose(0, 2, 1), SGU_CH, axis=2)
    bias_tab = _na_bias_tables(na_rpb)
    conv_b = mlstm_conv_b[:, None, :]

    for l in range(DEPTH):
        h = _ffn_call(h, gains, mods, w13_1, w2_1, l, 0)
        pm, pn, gates_t = _inproj_call(h, gains, mods, w_main, w_na, w_gate, l)
        a = _sgu_call(pm, sgu_norm_g[:, None, :], sgu_w, sgu_bias, l)
        qp, kt = _mlstm_prep_call(pm, mlstm_conv_w, conv_b, cos_t, sin_t, l)
        hf, hb = _mlstm_scan_call(qp, kt, pm, gates_t, gate_bias, l)
        cattn = _na_call(pn, bias_tab, l)
        h = _outproj_call(h, a, hf, hb, pm, mlstm_norm_g[:, None, :], cattn, w_out, mods, l)
        h = _ffn_call(h, gains, mods, w13_2, w2_2, l, 2)

    return _final_norm_call(h, final_norm_g[None, :])[None]
```

```python
import functools

import numpy as np
import jax
import jax.numpy as jnp
from jax import lax
from jax.experimental import pallas as pl
from jax.experimental.pallas import tpu as pltpu

F32 = jnp.float32
BF16 = jnp.bfloat16

D_MODEL = 2048
SEQ = 8192
DEPTH = 4
GRID_W = 64
CTX_LEN = 256
N_TOK = CTX_LEN + SEQ
D_FF = 5632
N_MOD = 9
EPS = 1e-6
ROPE_THETA = 10000.0
SGU_WIDTH = 512
SGU_GROUPS = 4
SGU_CH = SGU_WIDTH // SGU_GROUPS
SGU_CHUNK = 128
NA_WIDTH = 512
NA_HEADS = 8
NA_HD = NA_WIDTH // NA_HEADS
NA_WIN_H = 8
NA_WIN_W = 16
MLSTM_WIDTH = 1024
MLSTM_HEADS = 8
MLSTM_HD = MLSTM_WIDTH // MLSTM_HEADS
MLSTM_CHUNK = 128
MLSTM_CONV_W = 5
GRID_ROWS = SEQ // GRID_W
LANES = 128
PM_COLS = 2 * SGU_WIDTH + 4 * MLSTM_WIDTH
PN_COLS = 3 * NA_WIDTH
GATE_COLS = 4 * MLSTM_HEADS
NEG_BIG = -1e30

VMEM_LIMIT = 60 * 1024 * 1024
TM_FFN = 768
TM_IN = 1408
TM_OUT = 384
TF = 512
TN_MAIN = 512
TN_IN = 512
TM_PREP = 256
HALO = 8
NORM_ROWS = 16
SGU_STEP_CHUNKS = 6
SCAN_CHUNKS = 2
NA_ROWS = 4
NA_SLAB = 12
NA_VAR_TOP, NA_VAR_MID, NA_VAR_BOTTOM, NA_VAR_CTX = range(4)

NT_DIMS = (((1,), (1,)), ((), ()))


def _cparams(n_axes):
    return pltpu.CompilerParams(dimension_semantics=("arbitrary",) * n_axes,
                                vmem_limit_bytes=VMEM_LIMIT)


def _sigmoid(x):
    return 1.0 / (1.0 + jnp.exp(-x))


def _rms(x, g):
    return x * lax.rsqrt(jnp.mean(x * x, axis=-1, keepdims=True) + EPS) * g


def _is_ctx_rows(row0, tm):
    rows = row0 + lax.broadcasted_iota(jnp.int32, (tm, 1), 0)
    return rows < CTX_LEN


def _norm_modulate_store(x_ref, g_ref, mod_ref, xn_ref, rstd_ref, row0, tm):
    def stats(r, carry):
        rs = pl.multiple_of(r * 8, 8)
        x = x_ref[pl.ds(rs, 8), :]
        rstd = lax.rsqrt(jnp.mean(x * x, axis=-1, keepdims=True) + EPS)
        rstd_ref[pl.ds(rs, 8), :] = jnp.broadcast_to(rstd, (8, LANES))
        return carry

    lax.fori_loop(0, tm // 8, stats, 0, unroll=next(u for u in (24, 22, 16, 12, 11, 8) if (tm // 8) % u == 0))

    g = g_ref[...]
    lat_scale = g * (1.0 + mod_ref[1:2, :])
    ctx_scale = g * (1.0 + mod_ref[4:5, :])
    lat_shift = mod_ref[0:1, :]
    ctx_shift = mod_ref[3:4, :]

    def strip(r, carry):
        rs = pl.multiple_of(r * NORM_ROWS, NORM_ROWS)
        x = x_ref[pl.ds(rs, NORM_ROWS), :]
        rstd = rstd_ref[pl.ds(rs, NORM_ROWS), 0:1]
        is_ctx = row0 + rs < CTX_LEN
        scale = jnp.where(is_ctx, ctx_scale, lat_scale)
        shift = jnp.where(is_ctx, ctx_shift, lat_shift)
        xn_ref[pl.ds(rs, NORM_ROWS), :] = ((x * rstd) * scale + shift).astype(xn_ref.dtype)
        return carry

    lax.fori_loop(0, tm // NORM_ROWS, strip, 0, unroll=next(u for u in (4, 3, 2, 1) if (tm // NORM_ROWS) % u == 0))


def _mod_spec(l, s):
    return pl.BlockSpec((None, None, 8, D_MODEL), lambda *_: (l, s, 0, 0))


def _gain_spec(l, s):
    return pl.BlockSpec((None, None, 1, D_MODEL), lambda *_: (l, s, 0, 0))


def _ada_kernel(x_ref, w_ref, b_ref, o_ref):
    x = x_ref[...]
    xs = (x * _sigmoid(x)).astype(BF16)
    o_ref[0] = jnp.dot(xs, w_ref[0].astype(BF16), preferred_element_type=F32) + b_ref[0]


def _ada_call(cc, ada_w, ada_b):
    depth, d, nd = ada_w.shape
    tn = 1024
    return pl.pallas_call(
        _ada_kernel,
        grid=(depth, nd // tn),
        in_specs=[pl.BlockSpec((8, d), lambda l, j: (0, 0)),
                  pl.BlockSpec((1, d, tn), lambda l, j: (l, 0, j)),
                  pl.BlockSpec((1, 1, tn), lambda l, j: (l, 0, j))],
        out_specs=pl.BlockSpec((1, 8, tn), lambda l, j: (l, 0, j)),
        out_shape=jax.ShapeDtypeStruct((depth, 8, nd), F32),
        compiler_params=_cparams(2),
        name="ada_proj",
    )(cc, ada_w, ada_b.reshape(depth, 1, nd))


def _cast_kernel(x_ref, o_ref):
    o_ref[...] = x_ref[...].astype(o_ref.dtype)


def _cast_call(w, tr, tc, rows=None):
    depth, r, c = w.shape
    rows = r if rows is None else rows
    spec = pl.BlockSpec((1, tr, tc), lambda l, i, j: (l, i, j))
    return pl.pallas_call(
        _cast_kernel,
        grid=(depth, rows // tr, c // tc),
        in_specs=[spec],
        out_specs=spec,
        out_shape=jax.ShapeDtypeStruct((depth, rows, c), BF16),
        compiler_params=_cparams(3),
        name="cast_w",
    )(w)


def _ffn_kernel(x_ref, g_ref, mod_ref, w1_ref, w3_ref, w2_ref, o_ref, xn_ref, rstd_ref, *, tm, nk):
    i = pl.program_id(0)
    k = pl.program_id(1)

    @pl.when(k == 0)
    def _():
        _norm_modulate_store(x_ref, g_ref, mod_ref, xn_ref, rstd_ref, i * tm, tm)
        o_ref[...] = jnp.zeros_like(o_ref)

    xn = xn_ref[...]
    a = jnp.dot(xn, w1_ref[...], preferred_element_type=F32)
    g = jnp.dot(xn, w3_ref[...], preferred_element_type=F32)
    act = (a * _sigmoid(a) * g).astype(BF16)
    o_ref[...] += jnp.dot(act, w2_ref[...], preferred_element_type=F32)

    @pl.when(k == nk - 1)
    def _():
        gate = jnp.where(_is_ctx_rows(i * tm, tm), mod_ref[5:6, :], mod_ref[2:3, :])
        o_ref[...] = x_ref[...] + 0.5 * gate * o_ref[...]


def _ffn_call(h, gains, mods, w13, w2, l, s):
    n, d = h.shape
    nk = D_FF // TF
    tm = TM_FFN
    return pl.pallas_call(
        functools.partial(_ffn_kernel, tm=tm, nk=nk),
        grid=(n // tm, nk),
        in_specs=[pl.BlockSpec((tm, d), lambda i, k: (i, 0)),
                  _gain_spec(l, s),
                  _mod_spec(l, s),
                  pl.BlockSpec((None, d, TF), lambda i, k: (l, 0, k)),
                  pl.BlockSpec((None, d, TF), lambda i, k: (l, 0, k + nk)),
                  pl.BlockSpec((None, TF, d), lambda i, k: (l, k, 0))],
        out_specs=pl.BlockSpec((tm, d), lambda i, k: (i, 0)),
        out_shape=jax.ShapeDtypeStruct((n, d), F32),
        scratch_shapes=[pltpu.VMEM((tm, d), BF16), pltpu.VMEM((tm, LANES), F32)],
        compiler_params=_cparams(2),
        name="ffn",
    )(h, gains, mods, w13, w13, w2)


def _inproj_kernel(x_ref, g_ref, mod_ref, wm_ref, wn_ref, wg_ref, pm_ref, pn_ref, gt_ref, xn_ref, rstd_ref,
                   *, tm, n_main):
    i = pl.program_id(0)
    j = pl.program_id(1)

    @pl.when(j == 0)
    def _():
        _norm_modulate_store(x_ref, g_ref, mod_ref, xn_ref, rstd_ref, i * tm, tm)
        gt_ref[...] = lax.dot_general(wg_ref[...], xn_ref[...], NT_DIMS, preferred_element_type=F32)

    @pl.when(j < n_main)
    def _():
        pm_ref[...] = lax.dot_general(xn_ref[...], wm_ref[...], NT_DIMS, preferred_element_type=F32)

    @pl.when(j >= n_main)
    def _():
        pn_ref[...] = lax.dot_general(xn_ref[...], wn_ref[...], NT_DIMS,
                                      preferred_element_type=F32).astype(pn_ref.dtype)


def _inproj_call(h, gains, mods, w_main, w_na, w_gate, l):
    n, d = h.shape
    tm = TM_IN
    n_main = PM_COLS // TN_MAIN
    n_na = PN_COLS // TN_IN
    return pl.pallas_call(
        functools.partial(_inproj_kernel, tm=tm, n_main=n_main),
        grid=(n // tm, n_main + n_na),
        in_specs=[pl.BlockSpec((tm, d), lambda i, j: (i, 0)),
                  _gain_spec(l, 1),
                  _mod_spec(l, 1),
                  pl.BlockSpec((None, TN_MAIN, d), lambda i, j: (l, jnp.minimum(j, n_main - 1), 0)),
                  pl.BlockSpec((None, TN_IN, d), lambda i, j: (l, jnp.maximum(j - n_main, 0), 0)),
                  pl.BlockSpec((None, LANES, d), lambda i, j: (l, 0, 0))],
        out_specs=[pl.BlockSpec((tm, TN_MAIN), lambda i, j: (i, jnp.minimum(j, n_main - 1))),
                   pl.BlockSpec((tm, TN_IN), lambda i, j: (i, jnp.maximum(j - n_main, 0))),
                   pl.BlockSpec((LANES, tm), lambda i, j: (0, i))],
        out_shape=[jax.ShapeDtypeStruct((n, PM_COLS), F32),
                   jax.ShapeDtypeStruct((n, PN_COLS), BF16),
                   jax.ShapeDtypeStruct((LANES, n), F32)],
        scratch_shapes=[pltpu.VMEM((tm, d), BF16), pltpu.VMEM((tm, LANES), F32)],
        compiler_params=_cparams(2),
        name="mix_inproj",
    )(h, gains, mods, w_main, w_na, w_gate)


def _gelu_tanh(x):
    c = np.float32(np.sqrt(2.0 / np.pi))
    return x * (0.5 * (1.0 + jnp.tanh(c * (x + 0.044715 * (x * x * x)))))


def _sgu_kernel(u_ref, v_ref, g_ref, w_ref, b_ref, o_ref):
    c = SGU_CHUNK
    for ch in range(SGU_STEP_CHUNKS):
        rows = slice(ch * c, (ch + 1) * c)
        u = _gelu_tanh(u_ref[rows, :])
        v = _rms(_gelu_tanh(v_ref[rows, :]), g_ref[...]).astype(BF16)
        for grp in range(SGU_GROUPS):
            sl = slice(grp * SGU_CH, (grp + 1) * SGU_CH)
            mixed = jnp.dot(w_ref[grp].astype(BF16), v[:, sl], preferred_element_type=F32) + b_ref[:, sl]
            o_ref[rows, sl] = (u[:, sl] * mixed).astype(o_ref.dtype)


def _sgu_call(pm, norm_g, w_s, b_full, l):
    n = pm.shape[0]
    c = SGU_CHUNK
    tm = SGU_STEP_CHUNKS * c
    return pl.pallas_call(
        _sgu_kernel,
        grid=(n // tm,),
        in_specs=[pl.BlockSpec((tm, SGU_WIDTH), lambda i: (i, 0)),
                  pl.BlockSpec((tm, SGU_WIDTH), lambda i: (i, 1)),
                  pl.BlockSpec((None, 1, SGU_WIDTH), lambda i: (l, 0, 0)),
                  pl.BlockSpec((None, SGU_GROUPS, c, c), lambda i: (l, 0, 0, 0)),
                  pl.BlockSpec((None, c, SGU_WIDTH), lambda i: (l, 0, 0))],
        out_specs=pl.BlockSpec((tm, SGU_WIDTH), lambda i: (i, 0)),
        out_shape=jax.ShapeDtypeStruct((n, SGU_WIDTH), BF16),
        compiler_params=_cparams(1),
        name="sgu",
    )(pm, pm, norm_g, w_s, b_full)


def _prep_one(x_ref, prev_ref, next_ref, w_ref, b_ref, cos, sin, ext_ref, o_ref, zero_prev, zero_next,
              out_scale, tm, transposed):
    ext_ref[0:HALO, :] = jnp.where(zero_prev, 0.0, prev_ref[...])
    ext_ref[HALO:HALO + tm, :] = x_ref[...]
    ext_ref[HALO + tm:, :] = jnp.where(zero_next, 0.0, next_ref[...])
    half = MLSTM_CONV_W // 2
    y = b_ref[...] + w_ref[0:1, :] * ext_ref[HALO - half:HALO - half + tm, :]
    for j in range(1, MLSTM_CONV_W):
        y = y + w_ref[j:j + 1, :] * ext_ref[HALO - half + j:HALO - half + j + tm, :]
    y = y * _sigmoid(y)
    lane = lax.broadcasted_iota(jnp.int32, (tm, MLSTM_HD), 1)
    first = (lane % (MLSTM_HD // 2)) < (MLSTM_HD // 4)
    for hd in range(MLSTM_HEADS):
        sl = slice(hd * MLSTM_HD, (hd + 1) * MLSTM_HD)
        yh = y[:, sl]
        partner = jnp.where(first, pltpu.roll(yh, MLSTM_HD - MLSTM_HD // 4, 1),
                            pltpu.roll(yh, MLSTM_HD // 4, 1))
        val = (yh * cos + partner * sin) * out_scale
        if transposed:
            o_ref[sl, :] = val.T.astype(o_ref.dtype)
        else:
            o_ref[:, sl] = val.astype(o_ref.dtype)


def _mlstm_prep_kernel(q_ref, qp_ref, qn_ref, k_ref, kp_ref, kn_ref, wq_ref, wk_ref, bq_ref, bk_ref,
                       cos_ref, sin_ref, qo_ref, kt_ref, ext_ref, *, tm, n_tiles):
    i = pl.program_id(0)
    row0 = i * tm
    zero_prev = jnp.logical_or(row0 == 0, row0 == CTX_LEN)
    zero_next = jnp.logical_or(row0 + tm == CTX_LEN, i == n_tiles - 1)
    cos = cos_ref[...]
    sin = sin_ref[...]
    _prep_one(q_ref, qp_ref, qn_ref, wq_ref, bq_ref, cos, sin, ext_ref, qo_ref, zero_prev, zero_next,
              np.float32(MLSTM_HD ** -0.5), tm, False)
    _prep_one(k_ref, kp_ref, kn_ref, wk_ref, bk_ref, cos, sin, ext_ref, kt_ref, zero_prev, zero_next,
              np.float32(1.0), tm, True)


def _mlstm_prep_call(pm, conv_w, conv_b, cos_t, sin_t, l):
    n = pm.shape[0]
    tm = TM_PREP
    n_tiles = n // tm
    w = MLSTM_WIDTH
    hb = tm // HALO
    last_hb = n // HALO - 1
    q_col, k_col = 1, 2
    main = lambda col: pl.BlockSpec((tm, w), lambda i: (i, col))
    prev = lambda col: pl.BlockSpec((HALO, w), lambda i: (jnp.maximum(i * hb - 1, 0), col))
    nxt = lambda col: pl.BlockSpec((HALO, w), lambda i: (jnp.minimum((i + 1) * hb, last_hb), col))
    wspec = lambda col: pl.BlockSpec((None, MLSTM_CONV_W, w), lambda i: (l, 0, col))
    bspec = lambda col: pl.BlockSpec((None, 1, w), lambda i: (l, 0, col))
    tab = pl.BlockSpec((tm, MLSTM_HD), lambda i: (i, 0))
    return pl.pallas_call(
        functools.partial(_mlstm_prep_kernel, tm=tm, n_tiles=n_tiles),
        grid=(n_tiles,),
        in_specs=[main(q_col), prev(q_col), nxt(q_col), main(k_col), prev(k_col), nxt(k_col),
                  wspec(0), wspec(1), bspec(0), bspec(1), tab, tab],
        out_specs=[pl.BlockSpec((tm, w), lambda i: (i, 0)), pl.BlockSpec((w, tm), lambda i: (0, i))],
        out_shape=[jax.ShapeDtypeStruct((n, w), BF16), jax.ShapeDtypeStruct((w, n), BF16)],
        scratch_shapes=[pltpu.VMEM((tm + 2 * HALO, w), F32)],
        compiler_params=_cparams(1),
        name="mlstm_prep",
    )(pm, pm, pm, pm, pm, pm, conv_w, conv_w, conv_b, conv_b, cos_t, sin_t)


def _log_sigmoid(x):
    return jnp.minimum(x, 0.0) - jnp.log(1.0 + jnp.exp(-jnp.abs(x)))


def _split3(x):
    hi = x.astype(BF16)
    r1 = x - hi.astype(F32)
    mid = r1.astype(BF16)
    lo = (r1 - mid.astype(F32)).astype(BF16)
    return hi, mid, lo


def _mlstm_chunk(dd, r0, q_ref, kt_ref, v_ref, gr, o_ref, state):
    L = MLSTM_CHUNK
    H = MLSTM_HEADS
    hd = MLSTM_HD
    r_idx = lax.broadcasted_iota(jnp.int32, (L, L), 0)
    c_idx = lax.broadcasted_iota(jnp.int32, (L, L), 1)
    tri = (c_idx <= r_idx) if dd == 0 else (c_idx >= r_idx)
    tri_t = (c_idx >= r_idx) if dd == 0 else (c_idx <= r_idx)
    tri_b = jnp.where(tri, 1.0, 0.0).astype(BF16)
    tri_t_b = jnp.where(tri_t, 1.0, 0.0).astype(BF16)
    tri3 = jnp.concatenate([tri_b, tri_b, tri_b], axis=1)

    base = 2 * H * dd
    i_row = gr[base:base + H, :]
    lf_all = _log_sigmoid(gr[base:base + 2 * H, :])
    lf_row = lf_all[H:2 * H, :]
    parts = _split3(lf_all)
    b_row = sum(jnp.dot(p, tri_t_b, preferred_element_type=F32) for p in parts)[H:2 * H, :]
    b_cols = lax.dot_general(tri3, jnp.concatenate(parts, axis=1), NT_DIMS, preferred_element_type=F32)
    b_last = jnp.sum(lf_row, axis=-1, keepdims=True)
    ib_row = i_row - b_row
    g_row = b_last - b_row + i_row
    g_max = jnp.max(g_row, axis=-1, keepdims=True)

    new_state = []
    for h in range(H):
        sl = slice(h * hd, (h + 1) * hd)
        q = q_ref[r0:r0 + L, sl]
        kt = kt_ref[sl, r0:r0 + L]
        v = v_ref[r0:r0 + L, sl].astype(BF16)
        c_old, n_old, m_old = state[h]
        b_col = jnp.broadcast_to(b_cols[:, H + h:H + h + 1], (L, LANES))
        d_mat = jnp.where(tri, b_col + ib_row[h:h + 1, :], -jnp.inf)
        inter = b_col + m_old
        m_t = jnp.maximum(jnp.max(d_mat, axis=-1, keepdims=True), inter)
        qk = jnp.dot(q, kt, preferred_element_type=F32)
        s = qk * jnp.exp(d_mat - m_t)
        a = jnp.exp(inter - m_t)
        state_b = jnp.concatenate([c_old.astype(BF16), n_old.astype(BF16)], axis=1)
        qcn = jnp.dot(q, state_b, preferred_element_type=F32)
        bl = b_last[h:h + 1, :]
        m_new = jnp.maximum(bl + m_old, g_max[h:h + 1, :])
        w_row = jnp.exp(g_row[h:h + 1, :] - m_new)
        decay = jnp.exp(bl + m_old - m_new)
        kw = kt.astype(F32) * w_row
        sv = jnp.dot(jnp.concatenate([s.astype(BF16), kw.astype(BF16)], axis=0), v, preferred_element_type=F32)
        num = a * qcn[:, :hd] + sv[:L]
        den = a * qcn[:, hd:] + jnp.sum(s, axis=-1, keepdims=True)
        o_ref[r0:r0 + L, sl] = num * (1.0 / jnp.maximum(jnp.abs(den), jnp.exp(-m_t)))
        new_state.append((decay * c_old + sv[L:],
                          decay * n_old + jnp.sum(kw, axis=-1, keepdims=True),
                          m_new))
    return new_state


def _mlstm_scan_kernel(qf_ref, ktf_ref, vf_ref, gf_ref, qb_ref, ktb_ref, vb_ref, gb_ref, bias_ref,
                       of_ref, ob_ref, c_ref, n_ref, m_ref):
    @pl.when(pl.program_id(0) == 0)
    def _():
        c_ref[...] = jnp.zeros_like(c_ref)
        n_ref[...] = jnp.zeros_like(n_ref)
        m_ref[...] = jnp.zeros_like(m_ref)

    L = MLSTM_CHUNK
    H = MLSTM_HEADS
    bias = bias_ref[...]
    state = [[(c_ref[dd, h], n_ref[dd, h], m_ref[dd, h:h + 1, :]) for h in range(H)] for dd in range(2)]
    for c in range(SCAN_CHUNKS):
        rf = c * L
        rb = (SCAN_CHUNKS - 1 - c) * L
        state[0] = _mlstm_chunk(0, rf, qf_ref, ktf_ref, vf_ref, gf_ref[:, rf:rf + L] + bias, of_ref, state[0])
        state[1] = _mlstm_chunk(1, rb, qb_ref, ktb_ref, vb_ref, gb_ref[:, rb:rb + L] + bias, ob_ref, state[1])
    for dd in range(2):
        for h in range(H):
            c_ref[dd, h], n_ref[dd, h], m_ref[dd, h:h + 1, :] = state[dd][h]


def _mlstm_bwd_block(j, n_blocks):
    return jnp.where(j == 0, 0, n_blocks - j)


def _mlstm_scan_call(qp, kt, pm, gates_t, gate_bias, l):
    n = qp.shape[0]
    L = SCAN_CHUNKS * MLSTM_CHUNK
    assert L == CTX_LEN
    nc = n // L
    w = MLSTM_WIDTH
    v_col = 3
    fwd = lambda j: j
    bwd = lambda j: _mlstm_bwd_block(j, nc)

    def specs(cidx):
        return [pl.BlockSpec((L, w), lambda j: (cidx(j), 0)),
                pl.BlockSpec((w, L), lambda j: (0, cidx(j))),
                pl.BlockSpec((L, w), lambda j: (cidx(j), v_col)),
                pl.BlockSpec((LANES, L), lambda j: (0, cidx(j)))]

    state = lambda last: pltpu.VMEM((2, MLSTM_HEADS, MLSTM_HD, last), F32)
    return pl.pallas_call(
        _mlstm_scan_kernel,
        grid=(nc,),
        in_specs=specs(fwd) + specs(bwd) + [pl.BlockSpec((None, LANES, 1), lambda j: (l, 0, 0))],
        out_specs=[pl.BlockSpec((L, w), lambda j: (fwd(j), 0)),
                   pl.BlockSpec((L, w), lambda j: (bwd(j), 0))],
        out_shape=[jax.ShapeDtypeStruct((n, w), F32), jax.ShapeDtypeStruct((n, w), F32)],
        scratch_shapes=[state(MLSTM_HD), state(LANES), pltpu.VMEM((2, MLSTM_HEADS, LANES), F32)],
        compiler_params=_cparams(1),
        name="mlstm_scan",
    )(qp, kt, pm, gates_t, qp, kt, pm, gates_t, gate_bias)


def _na_block_geometry(t):
    r0 = jnp.maximum(t - 1, 0) * NA_ROWS
    s0 = jnp.clip(r0 - NA_WIN_H // 2, 0, GRID_ROWS - NA_SLAB)
    variant = jnp.where(t == 0, NA_VAR_CTX,
                        jnp.where(r0 == 0, NA_VAR_TOP, jnp.where(r0 == GRID_ROWS - NA_ROWS, NA_VAR_BOTTOM, NA_VAR_MID)))
    return s0, variant


def _na_kernel(q_ref, k_ref, v_ref, bias_ref, o_ref):
    t = pl.program_id(0)
    s0, _ = _na_block_geometry(t)
    nk = NA_SLAB * GRID_W
    start = pl.multiple_of(CTX_LEN + s0 * GRID_W, GRID_W)
    ks = k_ref[pl.ds(start, nk), :]
    vs = v_ref[pl.ds(start, nk), :]
    kc = k_ref[0:CTX_LEN, :]
    vc = v_ref[0:CTX_LEN, :]
    q = q_ref[...] * (NA_HD ** -0.5)
    lo = lax.broadcasted_iota(jnp.int32, (GRID_W, LANES), 1) < NA_HD
    for pr in range(NA_HEADS // 2):
        sl = slice(pr * LANES, (pr + 1) * LANES)
        pieces = []
        for rr in range(NA_ROWS):
            qp = q[rr * GRID_W:(rr + 1) * GRID_W, sl]
            zero = jnp.zeros_like(qp)
            pieces += [jnp.where(lo, qp, zero), jnp.where(lo, zero, qp)]
        qs = jnp.concatenate(pieces, axis=0)
        s_loc = lax.dot_general(qs, ks[:, sl], NT_DIMS, preferred_element_type=F32) + bias_ref[pr]
        s_ctx = lax.dot_general(qs, kc[:, sl], NT_DIMS, preferred_element_type=F32)
        m = jnp.maximum(jnp.max(s_loc, axis=-1, keepdims=True), jnp.max(s_ctx, axis=-1, keepdims=True))
        e_loc = jnp.exp(s_loc - m)
        e_ctx = jnp.exp(s_ctx - m)
        den = jnp.sum(e_loc, axis=-1, keepdims=True) + jnp.sum(e_ctx, axis=-1, keepdims=True)
        o = (jnp.dot(e_loc.astype(BF16), vs[:, sl], preferred_element_type=F32)
             + jnp.dot(e_ctx.astype(BF16), vc[:, sl], preferred_element_type=F32)) * (1.0 / den)
        for rr in range(NA_ROWS):
            top = o[2 * rr * GRID_W:(2 * rr + 1) * GRID_W]
            bot = o[(2 * rr + 1) * GRID_W:(2 * rr + 2) * GRID_W]
            o_ref[rr * GRID_W:(rr + 1) * GRID_W, sl] = jnp.where(lo, top, bot).astype(o_ref.dtype)


def _na_call(pn, bias_tab, l):
    n = pn.shape[0]
    nk = NA_SLAB * GRID_W
    tq = NA_ROWS * GRID_W
    assert tq == CTX_LEN and GRID_ROWS % NA_ROWS == 0
    return pl.pallas_call(
        _na_kernel,
        grid=(n // tq,),
        in_specs=[pl.BlockSpec((tq, NA_WIDTH), lambda t: (t, 0)),
                  pl.BlockSpec((n, NA_WIDTH), lambda t: (0, 1)),
                  pl.BlockSpec((n, NA_WIDTH), lambda t: (0, 2)),
                  pl.BlockSpec((None, None, NA_HEADS // 2, 2 * tq, nk),
                               lambda t: (l, _na_block_geometry(t)[1], 0, 0, 0))],
        out_specs=pl.BlockSpec((tq, NA_WIDTH), lambda t: (t, 0)),
        out_shape=jax.ShapeDtypeStruct((n, NA_WIDTH), BF16),
        compiler_params=_cparams(1),
        name="nbr_attn",
    )(pn, pn, pn, bias_tab)


def _na_bias_tables(rpb_all):
    depth = rpb_all.shape[0]
    cols = np.arange(GRID_W)
    col_start = np.clip(cols - NA_WIN_W // 2, 0, GRID_W - NA_WIN_W)
    in_win = (cols[None, :] >= col_start[:, None]) & (cols[None, :] < col_start[:, None] + NA_WIN_W)
    dc = cols[None, :] - cols[:, None] + (NA_WIN_W - 1)
    onehot = np.zeros((GRID_W, GRID_W, 2 * NA_WIN_W - 1), np.float32)
    cc, kk = np.nonzero(in_win)
    onehot[cc, kk, dc[cc, kk]] = 1.0
    mask_bias = np.where(in_win, 0.0, NEG_BIG).astype(np.float32)
    e = jnp.einsum('lhrd,ckd->lhcrk', rpb_all, jnp.asarray(onehot), precision=lax.Precision.HIGHEST)
    e = e + mask_bias[None, None, :, None, :]
    masked = jnp.full((depth, NA_HEADS, GRID_W, GRID_W), NEG_BIG, F32)
    first_rows = {NA_VAR_TOP: 0, NA_VAR_MID: 2 * NA_ROWS, NA_VAR_BOTTOM: GRID_ROWS - NA_ROWS}
    variants = []
    for var in (NA_VAR_TOP, NA_VAR_MID, NA_VAR_BOTTOM):
        r0 = first_rows[var]
        s0 = int(np.clip(r0 - NA_WIN_H // 2, 0, GRID_ROWS - NA_SLAB))
        per_row = []
        for rr in range(NA_ROWS):
            r = r0 + rr
            rs = int(np.clip(r - NA_WIN_H // 2, 0, GRID_ROWS - NA_WIN_H))
            blocks = []
            for j in range(NA_SLAB):
                key_row = s0 + j
                if rs <= key_row < rs + NA_WIN_H:
                    blocks.append(e[:, :, :, key_row - r + NA_WIN_H - 1, :])
                else:
                    blocks.append(masked)
            per_row.append(jnp.concatenate(blocks, axis=-1))
        variants.append(jnp.stack(per_row, axis=2))
    variants.append(jnp.full_like(variants[0], NEG_BIG))
    tab = jnp.stack(variants, axis=1)
    nk = NA_SLAB * GRID_W
    tab = tab.reshape(depth, 4, NA_HEADS // 2, 2, NA_ROWS, GRID_W, nk).transpose(0, 1, 2, 4, 3, 5, 6)
    return tab.reshape(depth, 4, NA_HEADS // 2, NA_ROWS * 2 * GRID_W, nk)


def _outproj_kernel(h_ref, a_ref, hf_ref, hb_ref, og_ref, ng_ref, c_ref, w_ref, mod_ref, o_ref, *, tm):
    i = pl.program_id(0)
    is_ctx = _is_ctx_rows(i * tm, tm)
    parts = []
    for hd in range(MLSTM_HEADS):
        sl = slice(hd * MLSTM_HD, (hd + 1) * MLSTM_HD)
        hh = _rms(hf_ref[:, sl] + hb_ref[:, sl], ng_ref[:, sl])
        parts.append((_sigmoid(og_ref[:, sl]) * hh).astype(BF16))
    b = jnp.concatenate(parts, axis=-1)
    y = jnp.dot(a_ref[...], w_ref[0:SGU_WIDTH, :], preferred_element_type=F32)
    y = y + jnp.dot(b, w_ref[SGU_WIDTH:SGU_WIDTH + MLSTM_WIDTH, :], preferred_element_type=F32)
    y = y + jnp.dot(c_ref[...], w_ref[SGU_WIDTH + MLSTM_WIDTH:, :], preferred_element_type=F32)
    gate = jnp.where(is_ctx, mod_ref[5:6, :], mod_ref[2:3, :])
    o_ref[...] = h_ref[...] + gate * y


def _outproj_call(h, a, hf, hb, pm, norm_g, c, w_out, mods, l):
    n, d = h.shape
    tm = TM_OUT
    w = MLSTM_WIDTH
    o_col = 4
    return pl.pallas_call(
        functools.partial(_outproj_kernel, tm=tm),
        grid=(n // tm,),
        in_specs=[pl.BlockSpec((tm, d), lambda i: (i, 0)),
                  pl.BlockSpec((tm, SGU_WIDTH), lambda i: (i, 0)),
                  pl.BlockSpec((tm, w), lambda i: (i, 0)),
                  pl.BlockSpec((tm, w), lambda i: (i, 0)),
                  pl.BlockSpec((tm, w), lambda i: (i, o_col)),
                  pl.BlockSpec((None, 1, w), lambda i: (l, 0, 0)),
                  pl.BlockSpec((tm, NA_WIDTH), lambda i: (i, 0)),
                  pl.BlockSpec((None, d, d), lambda i: (l, 0, 0)),
                  _mod_spec(l, 1)],
        out_specs=pl.BlockSpec((tm, d), lambda i: (i, 0)),
        out_shape=jax.ShapeDtypeStruct((n, d), F32),
        compiler_params=_cparams(1),
        name="mix_outproj",
    )(h, a, hf, hb, pm, norm_g, c, w_out, mods)


def _final_norm_kernel(x_ref, g_ref, o_ref):
    o_ref[...] = _rms(x_ref[...], g_ref[...])


def _final_norm_call(h, g):
    n, d = h.shape
    tm = CTX_LEN
    return pl.pallas_call(
        _final_norm_kernel,
        grid=(SEQ // tm,),
        in_specs=[pl.BlockSpec((tm, d), lambda i: (i + 1, 0)),
                  pl.BlockSpec((1, d), lambda i: (0, 0))],
        out_specs=pl.BlockSpec((tm, d), lambda i: (i, 0)),
        out_shape=jax.ShapeDtypeStruct((SEQ, d), F32),
        compiler_params=_cparams(1),
        name="final_norm",
    )(h, g)


def _rope_tables():
    nf = MLSTM_HD // 4
    freqs = ROPE_THETA ** (-jnp.arange(nf, dtype=F32) / nf)
    t = jnp.arange(SEQ)
    row = (t // GRID_W).astype(F32)
    col = (t % GRID_W).astype(F32)
    ang_r = row[:, None] * freqs
    ang_c = col[:, None] * freqs
    cos = jnp.concatenate([jnp.cos(ang_r), jnp.cos(ang_r), jnp.cos(ang_c), jnp.cos(ang_c)], axis=-1)
    sin = jnp.concatenate([-jnp.sin(ang_r), jnp.sin(ang_r), -jnp.sin(ang_c), jnp.sin(ang_c)], axis=-1)
    cos = jnp.concatenate([jnp.ones((CTX_LEN, MLSTM_HD), F32), cos], axis=0)
    sin = jnp.concatenate([jnp.zeros((CTX_LEN, MLSTM_HD), F32), sin], axis=0)
    return cos, sin


def _mod_table(mods):
    m = mods.reshape(DEPTH, 2, 3, 3, D_MODEL).transpose(0, 2, 1, 3, 4).reshape(DEPTH, 3, 6, D_MODEL)
    return jnp.pad(m, ((0, 0), (0, 0), (0, 2), (0, 0)))


def kernel(x, c, ctx, c_ctx, ada_w, ada_b, norm_g, ffn1_w13, ffn1_w2, ffn2_w13, ffn2_w2, mix_w_in, mix_w_out,
           sgu_norm_g, sgu_w, sgu_b, mlstm_conv_w, mlstm_conv_b, mlstm_gate_b, mlstm_norm_g, na_rpb,
           final_norm_g):
    assert x.shape == (1, SEQ, D_MODEL) and ctx.shape == (1, CTX_LEN, D_MODEL)
    h = jnp.concatenate([ctx[0], x[0]], axis=0)
    cc = jnp.concatenate([c, c_ctx[None, :], jnp.zeros((6, D_MODEL), F32)], axis=0)
    mods = _mod_table(_ada_call(cc, ada_w, ada_b)[:, :2].reshape(DEPTH, 2, N_MOD, D_MODEL))
    gains = norm_g[:, :, None, :]
    cos_t, sin_t = _rope_tables()

    w13_1 = _cast_call(ffn1_w13, D_MODEL, 1024)
    w13_2 = _cast_call(ffn2_w13, D_MODEL, 1024)
    w2_1 = _cast_call(ffn1_w2, D_FF // 4, D_MODEL)
    w2_2 = _cast_call(ffn2_w2, D_FF // 4, D_MODEL)
    w_out = _cast_call(mix_w_out, D_MODEL // 2, D_MODEL)
    w_in_t = mix_w_in.transpose(0, 2, 1)
    w_main = _cast_call(w_in_t, TN_MAIN, D_MODEL, rows=PM_COLS)
    w_na = _cast_call(w_in_t[:, PM_COLS + GATE_COLS:, :], TN_IN, D_MODEL)
    w_gate = jnp.pad(w_in_t[:, PM_COLS:PM_COLS + GATE_COLS, :],
                     ((0, 0), (0, LANES - GATE_COLS), (0, 0))).astype(BF16)
    gate_bias = jnp.pad(mlstm_gate_b.reshape(DEPTH, GATE_COLS), ((0, 0), (0, LANES - GATE_COLS)))[:, :, None]
    sgu_bias = jnp.repeat(sgu_b.transpose(0, 2, 1), SGU_CH, axis=2)
    bias_tab = _na_bias_tables(na_rpb)
    conv_b = mlstm_conv_b[:, None, :]

    for l in range(DEPTH):
        h = _ffn_call(h, gains, mods, w13_1, w2_1, l, 0)
        pm, pn, gates_t = _inproj_call(h, gains, mods, w_main, w_na, w_gate, l)
        a = _sgu_call(pm, sgu_norm_g[:, None, :], sgu_w, sgu_bias, l)
        qp, kt = _mlstm_prep_call(pm, mlstm_conv_w, conv_b, cos_t, sin_t, l)
        hf, hb = _mlstm_scan_call(qp, kt, pm, gates_t, gate_bias, l)
        cattn = _na_call(pn, bias_tab, l)
        h = _outproj_call(h, a, hf, hb, pm, mlstm_norm_g[:, None, :], cattn, w_out, mods, l)
        h = _ffn_call(h, gains, mods, w13_2, w2_2, l, 2)

    return _final_norm_call(h, final_norm_g[None, :])[None]
```
